```python
import math
import jax, jax.numpy as jnp
from jax import lax
import numpy as np

D_MODEL = 1024
BATCH = 4
SEQ = 4096
DEPTH = 4
DEC_BATCH = 2
DEC_SEQ = 8192
PAST_LEN = 128

GRID_W = 64
MAX_WIN_H = 8
WIN_W = 16
NA_HEADS = 8
NA_HEAD_DIM = 64
NA_WIDTH = NA_HEADS * NA_HEAD_DIM
DIFF_HEADS = 4
DIFF_QK_DIM = 64
DIFF_V_DIM = 2 * DIFF_QK_DIM
DIFF_WIDTH = DIFF_HEADS * DIFF_V_DIM
MIX_WIDTH = NA_WIDTH + DIFF_WIDTH
IN_WIDTH = 3 * NA_WIDTH + DIFF_HEADS * 2 * DIFF_QK_DIM * 2 + DIFF_WIDTH
D_FF = 2816
CONV_W = 3
ROPE_THETA = 10000.0
Q_BLOCK = 128
EPS = 1e-6

kernel_name = "hybrid_natten_diffattn_encoder"


def rms_norm(x, g):
    x32 = x.astype(jnp.float32)
    y = x32 * lax.rsqrt(jnp.mean(x32 * x32, axis=-1, keepdims=True) + EPS)
    return (y * g.astype(jnp.float32)).astype(x.dtype)


def rotary(x, T):
    d = x.shape[-1]
    inv_freq = 1.0 / (ROPE_THETA ** (jnp.arange(0, d, 2, dtype=jnp.float32) / d))
    ang = jnp.arange(T, dtype=jnp.float32)[:, None] * inv_freq[None, :]
    cos = jnp.concatenate([jnp.cos(ang), jnp.cos(ang)], -1).astype(x.dtype)[None, :, None, None, :]
    sin = jnp.concatenate([jnp.sin(ang), jnp.sin(ang)], -1).astype(x.dtype)[None, :, None, None, :]
    x1, x2 = x[..., : d // 2], x[..., d // 2:]
    rot = jnp.concatenate([-x2, x1], -1)
    return x * cos + rot * sin


def neighbourhood_attention(q, k, v, rpb):
    B, T, H, dh = q.shape
    rows = T // GRID_W
    win_h = min(MAX_WIN_H, rows)
    scale = 1.0 / math.sqrt(dh)
    qg = q.reshape(B, rows, GRID_W, H, dh).transpose(1, 0, 3, 2, 4)
    kg = k.reshape(B, rows, GRID_W, H, dh).transpose(0, 3, 1, 2, 4)
    vg = v.reshape(B, rows, GRID_W, H, dh).transpose(0, 3, 1, 2, 4)
    cols = np.arange(GRID_W)
    col_start = np.clip(cols - WIN_W // 2, 0, GRID_W - WIN_W)
    col_idx = col_start[:, None] + np.arange(WIN_W)[None, :]
    dc_idx = (col_idx - cols[:, None]) + (WIN_W - 1)
    rpb32 = rpb.astype(jnp.float32)

    def one_row(args):
        r, q_row = args
        start = jnp.clip(r - win_h // 2, 0, rows - win_h)
        k_band = lax.dynamic_slice_in_dim(kg, start, win_h, axis=2)
        v_band = lax.dynamic_slice_in_dim(vg, start, win_h, axis=2)
        k_win = k_band[:, :, :, col_idx, :]
        v_win = v_band[:, :, :, col_idx, :]
        dr_idx = start + jnp.arange(win_h) - r + (MAX_WIN_H - 1)
        bias = rpb32[:, dr_idx[None, :, None], dc_idx[:, None, :]]
        s = jnp.einsum('bhcd,bhwcjd->bhcwj', q_row, k_win).astype(jnp.float32) * scale + bias[None]
        p = jax.nn.softmax(s.reshape(B, H, GRID_W, win_h * WIN_W), axis=-1)
        p = p.reshape(B, H, GRID_W, win_h, WIN_W).astype(v.dtype)
        return jnp.einsum('bhcwj,bhwcjd->bhcd', p, v_win)

    out = lax.map(one_row, (jnp.arange(rows), qg))
    return out.transpose(1, 0, 3, 2, 4).reshape(B, T, H * dh)


def diff_attention(q, k, v, lam, subln_g, lam_init):
    B, T, H, _, d = q.shape
    dv = v.shape[-1]
    scale = 1.0 / math.sqrt(d)
    nb = T // Q_BLOCK
    k1 = k[..., 0, :].transpose(0, 2, 1, 3)
    k2 = k[..., 1, :].transpose(0, 2, 1, 3)
    vh = v.transpose(0, 2, 1, 3)
    qb = q.transpose(0, 2, 1, 3, 4).reshape(B, H, nb, Q_BLOCK, 2, d).transpose(2, 0, 1, 3, 4, 5)

    def one_block(qblk):
        s1 = jnp.einsum('bhqd,bhkd->bhqk', qblk[..., 0, :], k1).astype(jnp.float32) * scale
        s2 = jnp.einsum('bhqd,bhkd->bhqk', qblk[..., 1, :], k2).astype(jnp.float32) * scale
        w = jax.nn.softmax(s1, axis=-1) - lam * jax.nn.softmax(s2, axis=-1)
        return jnp.einsum('bhqk,bhkv->bhqv', w.astype(v.dtype), vh)

    out = lax.map(one_block, qb)
    out = out.transpose(1, 0, 3, 2, 4).reshape(B, T, H, dv)
    out = rms_norm(out, subln_g) * (1.0 - lam_init)
    return out.reshape(B, T, H * dv)


def dwconv_centred(h, w, b):
    hp = jnp.pad(h, ((0, 0), (1, 1), (0, 0)))
    return hp[:, :-2] * w[0] + hp[:, 1:-1] * w[1] + hp[:, 2:] * w[2] + b


def trunk(x, g_attn, w_in, rpb, lam_q1, lam_k1, lam_q2, lam_k2, subln_g, w_out,
          g_ffn, w_up, conv_w, conv_b, w_down, g_final):
    B, T, _ = x.shape
    o_qa, o_ka, o_va = 0, NA_WIDTH, 2 * NA_WIDTH
    o_qb = 3 * NA_WIDTH
    qk_b = DIFF_HEADS * 2 * DIFF_QK_DIM
    o_kb = o_qb + qk_b
    o_vb = o_kb + qk_b
    for l in range(DEPTH):
        n = rms_norm(x, g_attn[l])
        proj = n @ w_in[l]
        qa = proj[..., o_qa:o_ka].reshape(B, T, NA_HEADS, NA_HEAD_DIM)
        ka = proj[..., o_ka:o_va].reshape(B, T, NA_HEADS, NA_HEAD_DIM)
        va = proj[..., o_va:o_qb].reshape(B, T, NA_HEADS, NA_HEAD_DIM)
        qb = proj[..., o_qb:o_kb].reshape(B, T, DIFF_HEADS, 2, DIFF_QK_DIM)
        kb = proj[..., o_kb:o_vb].reshape(B, T, DIFF_HEADS, 2, DIFF_QK_DIM)
        vb = proj[..., o_vb:].reshape(B, T, DIFF_HEADS, DIFF_V_DIM)
        qb = rotary(qb, T)
        kb = rotary(kb, T)
        lam_init = 0.8 - 0.6 * math.exp(-0.3 * l)
        lam = (jnp.exp(jnp.sum(lam_q1[l].astype(jnp.float32) * lam_k1[l].astype(jnp.float32)))
               - jnp.exp(jnp.sum(lam_q2[l].astype(jnp.float32) * lam_k2[l].astype(jnp.float32)))
               + lam_init)
        ya = neighbourhood_attention(qa, ka, va, rpb[l])
        yb = diff_attention(qb, kb, vb, lam, subln_g[l], lam_init)
        x = x + jnp.concatenate([ya, yb], axis=-1) @ w_out[l]
        n = rms_norm(x, g_ffn[l])
        u = n @ w_up[l]
        gate, val = u[..., :D_FF], u[..., D_FF:]
        gate = dwconv_centred(gate, conv_w[l], conv_b[l])
        x = x + (jax.nn.gelu(gate, approximate=False) * val) @ w_down[l]
    return rms_norm(x, g_final)


def setup_inputs(seed: int = 0) -> dict:
    key = jax.random.key(seed)
    ks = jax.random.split(key, 20)
    f32 = jnp.float32
    nrm = lambda k, s, sc: jax.random.normal(k, s, f32) * sc
    return {
        "x_prompt": nrm(ks[0], (BATCH, SEQ, D_MODEL), 1.0),
        "x_sample": nrm(ks[1], (DEC_BATCH, DEC_SEQ, D_MODEL), 1.0),
        "g_attn": 1.0 + nrm(ks[2], (DEPTH, D_MODEL), 0.02),
        "w_in": nrm(ks[3], (DEPTH, D_MODEL, IN_WIDTH), D_MODEL ** -0.5),
        "rpb": nrm(ks[4], (DEPTH, NA_HEADS, 2 * MAX_WIN_H - 1, 2 * WIN_W - 1), 0.1),
        "lam_q1": nrm(ks[5], (DEPTH, DIFF_QK_DIM), 0.1),
        "lam_k1": nrm(ks[6], (DEPTH, DIFF_QK_DIM), 0.1),
        "lam_q2": nrm(ks[7], (DEPTH, DIFF_QK_DIM), 0.1),
        "lam_k2": nrm(ks[8], (DEPTH, DIFF_QK_DIM), 0.1),
        "subln_g": 1.0 + nrm(ks[9], (DEPTH, DIFF_V_DIM), 0.02),
        "w_out": nrm(ks[10], (DEPTH, MIX_WIDTH, D_MODEL), MIX_WIDTH ** -0.5),
        "g_ffn": 1.0 + nrm(ks[11], (DEPTH, D_MODEL), 0.02),
        "w_up": nrm(ks[12], (DEPTH, D_MODEL, 2 * D_FF), D_MODEL ** -0.5),
        "conv_w": nrm(ks[13], (DEPTH, CONV_W, D_FF), CONV_W ** -0.5),
        "conv_b": nrm(ks[14], (DEPTH, D_FF), 0.01),
        "w_down": nrm(ks[15], (DEPTH, D_FF, D_MODEL), D_FF ** -0.5),
        "g_final": 1.0 + nrm(ks[16], (D_MODEL,), 0.02),
    }


def reference(x_prompt, x_sample, g_attn, w_in, rpb, lam_q1, lam_k1, lam_q2, lam_k2,
              subln_g, w_out, g_ffn, w_up, conv_w, conv_b, w_down, g_final):
    y_prompt = trunk(x_prompt, g_attn, w_in, rpb, lam_q1, lam_k1, lam_q2, lam_k2, subln_g,
                     w_out, g_ffn, w_up, conv_w, conv_b, w_down, g_final)
    y_sample = trunk(x_sample, g_attn, w_in, rpb, lam_q1, lam_k1, lam_q2, lam_k2, subln_g,
                     w_out, g_ffn, w_up, conv_w, conv_b, w_down, g_final)
    return (y_prompt, y_sample)
```

```python
import functools
import math

import jax
import jax.numpy as jnp
import numpy as np
from jax import lax
from jax.experimental import pallas as pl
from jax.experimental.pallas import tpu as pltpu

F32 = jnp.float32
BF16 = jnp.bfloat16

D_MODEL = 1024
BATCH, SEQ = 4, 4096
DEC_BATCH, DEC_SEQ = 2, 8192
DEPTH = 4
GRID_W = 64
WIN_H, WIN_W = 8, 16
NA_HEADS, NA_HEAD_DIM = 8, 64
DIFF_HEADS, DIFF_QK_DIM, DIFF_V_DIM = 4, 64, 128
SECTION = 512
IN_WIDTH = 6 * SECTION
D_FF = 2816
EPS = 1e-6

N_PROMPT = BATCH * SEQ
N_SAMPLE = DEC_BATCH * DEC_SEQ
N_TOK = N_PROMPT + N_SAMPLE

LANES = 128
F32_SUBLANES = 8
VMEM_LIMIT_BYTES = 56 * 1024 * 1024

TM = 512
NA_QROWS = 4
NA_BROWS = 12
NA_Q = NA_QROWS * GRID_W
NA_K = NA_BROWS * GRID_W
NEG = -1e30
DIFF_TQ = 256
DIFF_TK = 512
FF_CHUNK = 256
HALO = F32_SUBLANES

_NT = (((1,), (1,)), ((), ()))


def _const_spec(shape):
    return pl.BlockSpec(shape, lambda *_: (0,) * len(shape), pipeline_mode=pl.Buffered(1))


def _params(*semantics):
    return pltpu.CompilerParams(dimension_semantics=semantics, vmem_limit_bytes=VMEM_LIMIT_BYTES)


def _rms(x, g):
    return x * lax.rsqrt(jnp.mean(x * x, axis=-1, keepdims=True) + EPS) * g


def _pos_block(i):
    tiles_prompt = N_PROMPT // TM
    return jnp.where(i < tiles_prompt, i % (SEQ // TM), (i - tiles_prompt) % (DEC_SEQ // TM))


def _inproj_kernel(x_ref, g_ref, w_ref, cos_ref, sin_lo_ref, sin_hi_ref,
                   qa_ref, ka_ref, va_ref, qb_ref, kb_ref, vb_ref):
    n = _rms(x_ref[...], g_ref[...]).astype(BF16)

    def proj(j):
        return jnp.dot(n, w_ref[:, j * SECTION:(j + 1) * SECTION], preferred_element_type=F32)

    qa_ref[...] = (proj(0) * (1.0 / math.sqrt(NA_HEAD_DIM))).astype(BF16)
    ka_ref[...] = proj(1).astype(BF16)
    va_ref[...] = proj(2).astype(BF16)
    cos, sin_lo, sin_hi = cos_ref[...], sin_lo_ref[...], sin_hi_ref[...]

    def rotary(x, scale):
        heads = []
        for h in range(DIFF_HEADS):
            xh = x[:, h * LANES:(h + 1) * LANES]
            r = xh * cos + pltpu.roll(xh, 96, 1) * sin_lo + pltpu.roll(xh, 32, 1) * sin_hi
            heads.append((r * scale).astype(BF16))
        return jnp.concatenate(heads, axis=1)

    qb_ref[...] = rotary(proj(3), 1.0 / math.sqrt(DIFF_QK_DIM))
    kb_ref[...] = rotary(proj(4), 1.0)
    vb_ref[...] = proj(5).astype(BF16)


def _inproj(x, g, w, cos, sin_lo, sin_hi):
    tok = pl.BlockSpec((TM, D_MODEL), lambda i: (i, 0))
    pos = pl.BlockSpec((TM, LANES), lambda i: (_pos_block(i), 0))
    sec = pl.BlockSpec((TM, SECTION), lambda i: (i, 0))
    return pl.pallas_call(
        _inproj_kernel,
        grid=(N_TOK // TM,),
        in_specs=[tok, _const_spec((1, D_MODEL)), _const_spec((D_MODEL, IN_WIDTH)), pos, pos, pos],
        out_specs=[sec] * 6,
        out_shape=[jax.ShapeDtypeStruct((N_TOK, SECTION), BF16)] * 6,
        compiler_params=_params("parallel"),
        name="inproj",
    )(x, g, w, cos, sin_lo, sin_hi)


def _rotary_tables():
    d = DIFF_QK_DIM
    inv_freq = 1.0 / (10000.0 ** (jnp.arange(0, d, 2, dtype=F32) / d))
    ang = jnp.arange(DEC_SEQ, dtype=F32)[:, None] * inv_freq[None, :]
    cos, sin, zero = jnp.cos(ang), jnp.sin(ang), jnp.zeros_like(ang)
    tile = lambda a, b: jnp.concatenate([a, b, a, b], axis=-1)
    return tile(cos, cos), tile(-sin, zero), tile(zero, sin)


def _na_patterns(rows):
    cols = np.arange(GRID_W)
    col_start = np.clip(cols - WIN_W // 2, 0, GRID_W - WIN_W)
    nblk = rows // NA_QROWS

    def one(j):
        r0 = j * NA_QROWS
        band = int(np.clip(r0 - WIN_H // 2, 0, rows - NA_BROWS))
        r = r0 + np.arange(NA_QROWS)
        start = np.clip(r - WIN_H // 2, 0, rows - WIN_H)
        kr = band + np.arange(NA_BROWS)
        row_ok = (kr[None, :] >= start[:, None]) & (kr[None, :] < start[:, None] + WIN_H)
        dr = kr[None, :] - r[:, None] + (WIN_H - 1)
        col_ok = (cols[None, :] >= col_start[:, None]) & (cols[None, :] < col_start[:, None] + WIN_W)
        dc = cols[None, :] - cols[:, None] + (WIN_W - 1)
        ok = row_ok[:, None, :, None] & col_ok[None, :, None, :]
        idx = dr[:, None, :, None] * (2 * WIN_W - 1) + dc[None, :, None, :]
        idx = np.where(ok, idx, 0)
        return idx.reshape(NA_Q, NA_K), ok.reshape(NA_Q, NA_K)

    pats = [one(0), one(1), one(nblk - 1)]
    for j in range(1, nblk - 1):
        ij, oj = one(j)
        assert (ij == pats[1][0]).all() and (oj == pats[1][1]).all()
    return np.stack([p[0] for p in pats]), np.stack([p[1] for p in pats])


def _na_bias_table(rpb_l):
    idx, ok = _na_patterns(SEQ // GRID_W)
    idx2, ok2 = _na_patterns(DEC_SEQ // GRID_W)
    assert (idx == idx2).all() and (ok == ok2).all()
    flat = rpb_l.astype(F32).reshape(NA_HEADS, -1)
    tab = jnp.where(jnp.asarray(ok)[None], flat[:, jnp.asarray(idx)], NEG)
    tab = tab.reshape(NA_HEADS // 2, 2, 3, NA_Q, NA_K).transpose(0, 2, 1, 3, 4)
    return tab.reshape(NA_HEADS // 2, 6, NA_Q, NA_K)


def _na_kernel(q_ref, k_ref, v_ref, bias_ref, o_ref, *, rows):
    nblk = rows // NA_QROWS
    lane = lax.broadcasted_iota(jnp.int32, (1, LANES), 1)
    first = lane < NA_HEAD_DIM

    def body(j, carry):
        r0 = j * NA_QROWS
        band = jnp.clip(r0 - WIN_H // 2, 0, rows - NA_BROWS)
        pat = jnp.where(j == 0, 0, jnp.where(j == nblk - 1, 2, 1))
        qoff = pl.multiple_of(r0 * GRID_W, NA_Q)
        koff = pl.multiple_of(band * GRID_W, GRID_W)
        q = q_ref[pl.ds(qoff, NA_Q), :]
        kb = k_ref[pl.ds(koff, NA_K), :]
        vb = v_ref[pl.ds(koff, NA_K), :]
        outs = []
        for hh in range(2):
            qm = jnp.where(first if hh == 0 else ~first, q, jnp.zeros_like(q))
            s = lax.dot_general(qm, kb, _NT, preferred_element_type=F32) + bias_ref[pat * 2 + hh]
            e = jnp.exp(s - jnp.max(s, axis=-1, keepdims=True))
            l = jnp.sum(e, axis=-1, keepdims=True)
            outs.append(jnp.dot(e.astype(BF16), vb, preferred_element_type=F32) / l)
        o_ref[pl.ds(qoff, NA_Q), :] = jnp.where(first, outs[0], outs[1]).astype(BF16)
        return carry

    lax.fori_loop(0, nblk, body, 0)


def _na_attention(qa, ka, va, bias, *, batch, seq, seq_block_offset):
    blk = pl.BlockSpec((seq, LANES), lambda hp, b: (b + seq_block_offset, hp))
    return pl.pallas_call(
        functools.partial(_na_kernel, rows=seq // GRID_W),
        grid=(NA_HEADS // 2, batch),
        in_specs=[blk, blk, blk, pl.BlockSpec((None, 6, NA_Q, NA_K), lambda hp, b: (hp, 0, 0, 0))],
        out_specs=pl.BlockSpec((seq, LANES), lambda hp, b: (b, hp)),
        out_shape=jax.ShapeDtypeStruct((batch * seq, SECTION), BF16),
        compiler_params=_params("parallel", "parallel"),
        name=f"na_attention_{seq}",
    )(qa, ka, va, bias)


def _diff_kernel(lq1_ref, lk1_ref, lq2_ref, lk2_ref, g_ref, q_ref, k_ref, v_ref, o_ref, *, seq, lam_init):
    lane = lax.broadcasted_iota(jnp.int32, (1, LANES), 1)
    first = lane < DIFF_QK_DIM
    q = q_ref[...]
    zero = jnp.zeros_like(q)
    qs = (jnp.where(first, q, zero), jnp.where(first, zero, q))

    def body(c, carry):
        off = pl.multiple_of(c * DIFF_TK, DIFF_TK)
        kc = k_ref[pl.ds(off, DIFF_TK), :]
        vc = v_ref[pl.ds(off, DIFF_TK), :]
        new = []
        for comp in range(2):
            m, l, a = carry[comp]
            s = lax.dot_general(qs[comp], kc, _NT, preferred_element_type=F32)
            m_new = jnp.maximum(m, jnp.max(s, axis=-1, keepdims=True))
            alpha = jnp.exp(m - m_new)
            p = jnp.exp(s - m_new)
            l = alpha * l + jnp.sum(p, axis=-1, keepdims=True)
            a = alpha * a + jnp.dot(p.astype(BF16), vc, preferred_element_type=F32)
            new.append((m_new, l, a))
        return tuple(new)

    init = (jnp.full((DIFF_TQ, 1), NEG, F32), jnp.zeros((DIFF_TQ, 1), F32), jnp.zeros((DIFF_TQ, LANES), F32))
    (_, l1, a1), (_, l2, a2) = lax.fori_loop(0, seq // DIFF_TK, body, (init, init))

    lam = (jnp.exp(jnp.sum(lq1_ref[...] * lk1_ref[...], axis=-1, keepdims=True))
           - jnp.exp(jnp.sum(lq2_ref[...] * lk2_ref[...], axis=-1, keepdims=True)) + lam_init)
    out = a1 / l1 - lam * (a2 / l2)
    o_ref[...] = (_rms(out, g_ref[...]) * (1.0 - lam_init)).astype(BF16)


def _diff_attention(lams, g, qb, kb, vb, *, batch, seq, lam_init):
    nq = seq // DIFF_TQ
    q_off = 0 if seq == SEQ else N_PROMPT // DIFF_TQ
    s_off = 0 if seq == SEQ else N_PROMPT // seq
    vec = _const_spec((1, DIFF_QK_DIM))
    kv = pl.BlockSpec((seq, LANES), lambda b, h, i: (b + s_off, h))
    return pl.pallas_call(
        functools.partial(_diff_kernel, seq=seq, lam_init=lam_init),
        grid=(batch, DIFF_HEADS, nq),
        in_specs=[vec, vec, vec, vec, _const_spec((1, DIFF_V_DIM)),
                  pl.BlockSpec((DIFF_TQ, LANES), lambda b, h, i: (b * nq + i + q_off, h)), kv, kv],
        out_specs=pl.BlockSpec((DIFF_TQ, LANES), lambda b, h, i: (b * nq + i, h)),
        out_shape=jax.ShapeDtypeStruct((batch * seq, SECTION), BF16),
        compiler_params=_params("parallel", "parallel", "arbitrary"),
        name=f"diff_attention_{seq}",
    )(*lams, g, qb, kb, vb)


def _outproj_kernel(x_ref, ya_ref, yb_ref, w_ref, o_ref):
    o_ref[...] = (x_ref[...]
                  + jnp.dot(ya_ref[...], w_ref[:SECTION, :], preferred_element_type=F32)
                  + jnp.dot(yb_ref[...], w_ref[SECTION:, :], preferred_element_type=F32))


def _outproj(x, ya, yb, w):
    tok = pl.BlockSpec((TM, D_MODEL), lambda i: (i, 0))
    sec = pl.BlockSpec((TM, SECTION), lambda i: (i, 0))
    return pl.pallas_call(
        _outproj_kernel,
        grid=(N_TOK // TM,),
        in_specs=[tok, sec, sec, _const_spec((2 * SECTION, D_MODEL))],
        out_specs=tok,
        out_shape=jax.ShapeDtypeStruct((N_TOK, D_MODEL), F32),
        compiler_params=_params("parallel"),
        name="outproj",
    )(x, ya, yb, w)


def _ffn_kernel(x_ref, xp_ref, xn_ref, g_ref, wup_ref, cw_ref, cb_ref, wdn_ref, gf_ref, o_ref, h_ref, *, final):
    i = pl.program_id(0)
    tiles_prompt = N_PROMPT // TM
    per_seq = jnp.where(i < tiles_prompt, SEQ // TM, DEC_SEQ // TM)
    has_prev = (i % per_seq) != 0
    has_next = (i % per_seq) != per_seq - 1

    x = x_ref[...]
    xe = jnp.concatenate([xp_ref[...], x, xn_ref[...]], axis=0)
    n = _rms(xe, g_ref[...]).astype(BF16)
    ext = TM + 2 * HALO
    row = lax.broadcasted_iota(jnp.int32, (ext, 1), 0)
    keep = ((row >= HALO) | has_prev) & ((row < HALO + TM) | has_next)

    for c in range(D_FF // FF_CHUNK):
        lo = c * FF_CHUNK
        gate = jnp.dot(n, wup_ref[:, lo:lo + FF_CHUNK], preferred_element_type=F32)
        val = jnp.dot(n, wup_ref[:, D_FF + lo:D_FF + lo + FF_CHUNK], preferred_element_type=F32)
        gate = jnp.where(keep, gate, 0.0)
        cw = cw_ref[:, lo:lo + FF_CHUNK]
        conv = (pltpu.roll(gate, 1, 0) * cw[0:1] + gate * cw[1:2] + pltpu.roll(gate, ext - 1, 0) * cw[2:3]
                + cb_ref[:, lo:lo + FF_CHUNK])
        act = 0.5 * conv * (1.0 + lax.erf(conv * (1.0 / math.sqrt(2.0))))
        h_ref[:, lo:lo + FF_CHUNK] = (act * val)[HALO:HALO + TM].astype(BF16)

    y = x + jnp.dot(h_ref[...], wdn_ref[...], preferred_element_type=F32)
    o_ref[...] = _rms(y, gf_ref[...]) if final else y


def _ffn(x, g, wup, cw, cb, wdn, gf, *, final):
    tok = pl.BlockSpec((TM, D_MODEL), lambda i: (i, 0))
    blocks_per_tile = TM // HALO
    prev = pl.BlockSpec((HALO, D_MODEL), lambda i: (jnp.maximum(i * blocks_per_tile - 1, 0), 0))
    nxt = pl.BlockSpec((HALO, D_MODEL), lambda i: (jnp.minimum((i + 1) * blocks_per_tile, N_TOK // HALO - 1), 0))
    return pl.pallas_call(
        functools.partial(_ffn_kernel, final=final),
        grid=(N_TOK // TM,),
        in_specs=[tok, prev, nxt, _const_spec((1, D_MODEL)), _const_spec((D_MODEL, 2 * D_FF)),
                  _const_spec((3, D_FF)), _const_spec((1, D_FF)), _const_spec((D_FF, D_MODEL)),
                  _const_spec((1, D_MODEL))],
        out_specs=tok,
        out_shape=jax.ShapeDtypeStruct((N_TOK, D_MODEL), F32),
        scratch_shapes=[pltpu.VMEM((TM, D_FF), BF16)],
        compiler_params=_params("parallel"),
        name="ffn_final" if final else "ffn",
    )(x, x, x, g, wup, cw, cb, wdn, gf)


def kernel(x_prompt, x_sample, g_attn, w_in, rpb, lam_q1, lam_k1, lam_q2, lam_k2, subln_g, w_out, g_ffn, w_up,
           conv_w, conv_b, w_down, g_final):
    x = jnp.concatenate([x_prompt.reshape(N_PROMPT, D_MODEL), x_sample.reshape(N_SAMPLE, D_MODEL)], axis=0)
    cos, sin_lo, sin_hi = _rotary_tables()
    row = lambda v: v.reshape(1, -1).astype(F32)
    for l in range(DEPTH):
        lam_init = 0.8 - 0.6 * math.exp(-0.3 * l)
        qa, ka, va, qb, kb, vb = _inproj(x, row(g_attn[l]), w_in[l].astype(BF16), cos, sin_lo, sin_hi)
        bias = _na_bias_table(rpb[l])
        lams = (row(lam_q1[l]), row(lam_k1[l]), row(lam_q2[l]), row(lam_k2[l]))
        ya, yb = [], []
        for batch, seq in ((BATCH, SEQ), (DEC_BATCH, DEC_SEQ)):
            ya.append(_na_attention(qa, ka, va, bias, batch=batch, seq=seq,
                                    seq_block_offset=0 if seq == SEQ else N_PROMPT // seq))
            yb.append(_diff_attention(lams, row(subln_g[l]), qb, kb, vb, batch=batch, seq=seq, lam_init=lam_init))
        x = _outproj(x, jnp.concatenate(ya, axis=0), jnp.concatenate(yb, axis=0), w_out[l].astype(BF16))
        x = _ffn(x, row(g_ffn[l]), w_up[l].astype(BF16), conv_w[l].astype(F32), row(conv_b[l]),
                 w_down[l].astype(BF16), row(g_final), final=(l == DEPTH - 1))
    return (x[:N_PROMPT].reshape(BATCH, SEQ, D_MODEL), x[N_PROMPT:].reshape(DEC_BATCH, DEC_SEQ, D_MODEL))
```

```python
import functools
import math

import jax
import jax.numpy as jnp
import numpy as np
from jax import lax
from jax.experimental import pallas as pl
from jax.experimental.pallas import tpu as pltpu

F32 = jnp.float32
BF16 = jnp.bfloat16

D_MODEL = 1024
BATCH, SEQ = 4, 4096
DEC_BATCH, DEC_SEQ = 2, 8192
DEPTH = 4
GRID_W = 64
WIN_H, WIN_W = 8, 16
NA_HEADS, NA_HEAD_DIM = 8, 64
DIFF_HEADS, DIFF_QK_DIM, DIFF_V_DIM = 4, 64, 128
SECTION = 512
IN_WIDTH = 6 * SECTION
D_FF = 2816
EPS = 1e-6

N_PROMPT = BATCH * SEQ
N_SAMPLE = DEC_BATCH * DEC_SEQ
N_TOK = N_PROMPT + N_SAMPLE
SEQ_BLOCKS = N_TOK // DEC_SEQ
PROMPT_BLOCKS = N_PROMPT // DEC_SEQ
assert DEC_SEQ == 2 * SEQ and N_PROMPT % DEC_SEQ == 0

LANES = 128
F32_SUBLANES = 8
VMEM_LIMIT_BYTES = 56 * 1024 * 1024

TM = 512
NA_QROWS = 4
NA_BROWS = 12
NA_Q = NA_QROWS * GRID_W
NA_K = NA_BROWS * GRID_W
NEG = -1e30
DIFF_TQ = 256
DIFF_TK = 1024
DIFF_SAFE_SCORE = 50.0
FF_CHUNK = 256
HALO = F32_SUBLANES

_NT = (((1,), (1,)), ((), ()))


def _const_spec(shape):
    return pl.BlockSpec(shape, lambda *_: (0,) * len(shape), pipeline_mode=pl.Buffered(1))


def _params(*semantics):
    return pltpu.CompilerParams(dimension_semantics=semantics, vmem_limit_bytes=VMEM_LIMIT_BYTES)


def _rms(x, g):
    return x * lax.rsqrt(jnp.mean(x * x, axis=-1, keepdims=True) + EPS) * g


def _split_heads(x):
    first = lax.broadcasted_iota(jnp.int32, (1, LANES), 1) < LANES // 2
    zero = jnp.zeros_like(x)
    return jnp.concatenate([jnp.where(first, x, zero), jnp.where(first, zero, x)], axis=0)


def _pos_block(i):
    tiles_prompt = N_PROMPT // TM
    return jnp.where(i < tiles_prompt, i % (SEQ // TM), (i - tiles_prompt) % (DEC_SEQ // TM))


def _inproj_kernel(x_ref, g_ref, w_ref, cos_ref, sin_lo_ref, sin_hi_ref,
                   qa_ref, ka_ref, va_ref, qb_ref, kb_ref, vb_ref):
    n = _rms(x_ref[...], g_ref[...]).astype(BF16)

    def proj(j):
        return jnp.dot(n, w_ref[:, j * SECTION:(j + 1) * SECTION], preferred_element_type=F32)

    qa_ref[...] = (proj(0) * (1.0 / math.sqrt(NA_HEAD_DIM))).astype(BF16)
    ka_ref[...] = proj(1).astype(BF16)
    va_ref[...] = proj(2).astype(BF16)
    cos, sin_lo, sin_hi = cos_ref[...], sin_lo_ref[...], sin_hi_ref[...]

    def rotary(x, scale):
        heads = []
        for h in range(DIFF_HEADS):
            xh = x[:, h * LANES:(h + 1) * LANES]
            r = xh * cos + pltpu.roll(xh, 96, 1) * sin_lo + pltpu.roll(xh, 32, 1) * sin_hi
            heads.append((r * scale).astype(BF16))
        return jnp.concatenate(heads, axis=1)

    qb_ref[...] = rotary(proj(3), 1.0 / math.sqrt(DIFF_QK_DIM))
    kb_ref[...] = rotary(proj(4), 1.0)
    vb_ref[...] = proj(5).astype(BF16)


def _inproj(x, g, w, cos, sin_lo, sin_hi):
    tok = pl.BlockSpec((TM, D_MODEL), lambda i: (i, 0))
    pos = pl.BlockSpec((TM, LANES), lambda i: (_pos_block(i), 0))
    sec = pl.BlockSpec((TM, SECTION), lambda i: (i, 0))
    return pl.pallas_call(
        _inproj_kernel,
        grid=(N_TOK // TM,),
        in_specs=[tok, _const_spec((1, D_MODEL)), _const_spec((D_MODEL, IN_WIDTH)), pos, pos, pos],
        out_specs=[sec] * 6,
        out_shape=[jax.ShapeDtypeStruct((N_TOK, SECTION), BF16)] * 6,
        compiler_params=_params("parallel"),
        name="inproj",
    )(x, g, w, cos, sin_lo, sin_hi)


def _rotary_tables():
    d = DIFF_QK_DIM
    inv_freq = 1.0 / (10000.0 ** (jnp.arange(0, d, 2, dtype=F32) / d))
    ang = jnp.arange(DEC_SEQ, dtype=F32)[:, None] * inv_freq[None, :]
    cos, sin, zero = jnp.cos(ang), jnp.sin(ang), jnp.zeros_like(ang)
    tile = lambda a, b: jnp.concatenate([a, b, a, b], axis=-1)
    return tile(cos, cos), tile(-sin, zero), tile(zero, sin)


def _na_row_patterns(rows):
    nblk = rows // NA_QROWS

    def one(j):
        r0 = j * NA_QROWS
        band = int(np.clip(r0 - WIN_H // 2, 0, rows - NA_BROWS))
        r = r0 + np.arange(NA_QROWS)
        start = np.clip(r - WIN_H // 2, 0, rows - WIN_H)
        kr = band + np.arange(NA_BROWS)
        ok = (kr[None, :] >= start[:, None]) & (kr[None, :] < start[:, None] + WIN_H)
        dr = kr[None, :] - r[:, None] + (WIN_H - 1)
        return ok, np.where(ok, dr, 0)

    pats = [one(0), one(1), one(nblk - 1)]
    for j in range(1, nblk - 1):
        assert all((a == b).all() for a, b in zip(one(j), pats[1]))
    return np.stack([p[0] for p in pats]), np.stack([p[1] for p in pats])


def _na_col_table(rpb_l):
    cols = np.arange(GRID_W)
    col_start = np.clip(cols - WIN_W // 2, 0, GRID_W - WIN_W)
    ok = (cols[None, :] >= col_start[:, None]) & (cols[None, :] < col_start[:, None] + WIN_W)
    dc = cols[None, :] - cols[:, None] + (WIN_W - 1)
    onehot = ((dc[None] == np.arange(2 * WIN_W - 1)[:, None, None]) & ok[None]).astype(np.float32)
    t = jnp.einsum("hrd,dck->hrck", rpb_l.astype(F32), jnp.asarray(onehot), precision=lax.Precision.HIGHEST)
    t = jnp.where(jnp.asarray(ok), t, NEG)
    return jnp.concatenate([t, t], axis=-1)


def _na_kernel(q_ref, k_ref, v_ref, tt_ref, o_ref, bias_ref, s_ref):
    blk = pl.program_id(1)

    @pl.when(blk == 0)
    def _():
        ok, dr = _na_row_patterns(SEQ // GRID_W)
        ok2, dr2 = _na_row_patterns(DEC_SEQ // GRID_W)
        assert (ok == ok2).all() and (dr == dr2).all()
        neg = jnp.full((GRID_W, GRID_W), NEG, F32)
        for pat in range(3):
            for hh in range(2):
                for qr in range(NA_QROWS):
                    r = hh * NA_Q + qr * GRID_W
                    for w in range(NA_BROWS):
                        half = (w % 2) * GRID_W
                        blockval = tt_ref[hh, int(dr[pat, qr, w]), :, half:half + GRID_W] if ok[pat, qr, w] else neg
                        bias_ref[pat, r:r + GRID_W, w * GRID_W:(w + 1) * GRID_W] = blockval

    is_prompt = blk < PROMPT_BLOCKS
    rows = jnp.where(is_prompt, SEQ // GRID_W, DEC_SEQ // GRID_W)
    nblk_p, nblk_s = SEQ // GRID_W // NA_QROWS, DEC_SEQ // GRID_W // NA_QROWS
    first = lax.broadcasted_iota(jnp.int32, (1, LANES), 1) < NA_HEAD_DIM

    def place(j):
        jr = jnp.where(is_prompt, j % nblk_p, j)
        base = jnp.where(is_prompt, (j // nblk_p) * (SEQ // GRID_W), 0)
        last = jnp.where(is_prompt, nblk_p, nblk_s) - 1
        r0 = jr * NA_QROWS
        band = jnp.clip(r0 - WIN_H // 2, 0, rows - NA_BROWS)
        pat = jnp.where(jr == 0, 0, jnp.where(jr == last, 2, 1))
        qoff = pl.multiple_of((base + r0) * GRID_W, NA_Q)
        koff = pl.multiple_of((base + band) * GRID_W, GRID_W)
        return qoff, koff, pat

    def scores(j, slot):
        qoff, koff, pat = place(j)
        qs = _split_heads(q_ref[pl.ds(qoff, NA_Q), :])
        s_ref[slot] = (lax.dot_general(qs, k_ref[pl.ds(koff, NA_K), :], _NT, preferred_element_type=F32)
                       + bias_ref[pat])

    def finish(j, slot):
        qoff, koff, _ = place(j)
        s = s_ref[slot]
        e = jnp.exp(s - jnp.max(s, axis=-1, keepdims=True))
        l = jnp.sum(e, axis=-1, keepdims=True)
        o = jnp.dot(e.astype(BF16), v_ref[pl.ds(koff, NA_K), :], preferred_element_type=F32) / l
        o_ref[pl.ds(qoff, NA_Q), :] = jnp.where(first, o[:NA_Q], o[NA_Q:]).astype(BF16)

    def body(g, carry):
        j = 2 * g
        scores(j + 1, 1)
        finish(j, 0)
        scores(j + 2, 0)
        finish(j + 1, 1)
        return carry

    scores(0, 0)
    lax.fori_loop(0, nblk_s // 2 - 1, body, 0)
    scores(nblk_s - 1, 1)
    finish(nblk_s - 2, 0)
    finish(nblk_s - 1, 1)


def _na_attention(qa, ka, va, tt):
    blk = pl.BlockSpec((DEC_SEQ, LANES), lambda hp, b: (b, hp))
    return pl.pallas_call(
        _na_kernel,
        grid=(NA_HEADS // 2, SEQ_BLOCKS),
        in_specs=[blk, blk, blk,
                  pl.BlockSpec((2, 2 * WIN_H - 1, GRID_W, LANES), lambda hp, b: (hp, 0, 0, 0))],
        out_specs=blk,
        out_shape=jax.ShapeDtypeStruct((N_TOK, SECTION), BF16),
        scratch_shapes=[pltpu.VMEM((3, 2 * NA_Q, NA_K), F32), pltpu.VMEM((2, 2 * NA_Q, NA_K), F32)],
        compiler_params=_params("arbitrary", "arbitrary"),
        name="na_attention",
    )(qa, ka, va, tt)


def _half_sums(x):
    first = lax.broadcasted_iota(jnp.int32, (1, LANES), 1) < LANES // 2
    zero = jnp.zeros_like(x)
    return (jnp.sum(jnp.where(first, x, zero), axis=-1, keepdims=True),
            jnp.sum(jnp.where(first, zero, x), axis=-1, keepdims=True))


def _diff_kernel(lq1_ref, lk1_ref, lq2_ref, lk2_ref, g_ref, q_ref, k_ref, v_ref, o_ref, knorm_ref, m_ref,
                 s_ref, acc_ref, *, lam_init):
    blk, i = pl.program_id(0), pl.program_id(2)
    is_prompt = blk < PROMPT_BLOCKS
    qblocks_p = SEQ // DIFF_TQ
    i_seq = jnp.where(is_prompt, i % qblocks_p, i)
    kbase = jnp.where(is_prompt, (i // qblocks_p) * SEQ, 0)
    nchunks = jnp.where(is_prompt, SEQ // DIFF_TK, DEC_SEQ // DIFF_TK)

    def kv_chunk(ref, c):
        return ref[pl.ds(pl.multiple_of(kbase + c * DIFF_TK, DIFF_TK), DIFF_TK), :]

    @pl.when(i_seq == 0)
    def _():
        def body(c, carry):
            kc = kv_chunk(k_ref, c).astype(F32)
            n1, n2 = _half_sums(kc * kc)
            return (jnp.maximum(carry[0], jnp.max(n1, axis=0, keepdims=True)),
                    jnp.maximum(carry[1], jnp.max(n2, axis=0, keepdims=True)))
        zero = jnp.zeros((1, 1), F32)
        k1, k2 = lax.fori_loop(0, nchunks, body, (zero, zero))
        knorm_ref[0:1, :] = jnp.broadcast_to(k1, (1, LANES))
        knorm_ref[1:2, :] = jnp.broadcast_to(k2, (1, LANES))

    q = q_ref[...]
    qs = _split_heads(q)

    qf = q.astype(F32)
    qn1, qn2 = _half_sums(qf * qf)
    bound_sq = jnp.maximum(jnp.max(qn1, axis=0, keepdims=True) * knorm_ref[0:1, 0:1],
                           jnp.max(qn2, axis=0, keepdims=True) * knorm_ref[1:2, 0:1])
    needs_max = jnp.max(bound_sq) > DIFF_SAFE_SCORE ** 2

    m_ref[...] = jnp.zeros_like(m_ref)

    @pl.when(needs_max)
    def _():
        def body(c, m):
            s = lax.dot_general(qs, kv_chunk(k_ref, c), _NT, preferred_element_type=F32)
            return jnp.maximum(m, jnp.max(s, axis=-1, keepdims=True))
        m = lax.fori_loop(0, nchunks, body, jnp.full((2 * DIFF_TQ, 1), NEG, F32))
        m_ref[...] = jnp.broadcast_to(m, m_ref.shape)

    m = m_ref[...]
    ones = jnp.ones((DIFF_TK, LANES), BF16)

    def scores(c, slot):
        s_ref[slot] = lax.dot_general(qs, kv_chunk(k_ref, c), _NT, preferred_element_type=F32)

    def accumulate(c, slot):
        p = jnp.concatenate([jnp.exp(s_ref[slot, :, t * LANES:(t + 1) * LANES] - m)
                             for t in range(DIFF_TK // LANES)], axis=1).astype(BF16)
        acc_ref[...] += jnp.dot(p, jnp.concatenate([kv_chunk(v_ref, c), ones], axis=1),
                                preferred_element_type=F32)

    def body(g, carry):
        c = 2 * g
        scores(c + 1, 1)
        accumulate(c, 0)
        scores(c + 2, 0)
        accumulate(c + 1, 1)
        return carry

    acc_ref[...] = jnp.zeros_like(acc_ref)
    scores(0, 0)
    lax.fori_loop(0, nchunks // 2 - 1, body, 0)
    scores(nchunks - 1, 1)
    accumulate(nchunks - 2, 0)
    accumulate(nchunks - 1, 1)
    acc = acc_ref[...]
    sm1 = acc[:DIFF_TQ, :LANES] / acc[:DIFF_TQ, LANES:]
    sm2 = acc[DIFF_TQ:, :LANES] / acc[DIFF_TQ:, LANES:]
    lam = (jnp.exp(jnp.sum(lq1_ref[...] * lk1_ref[...], axis=-1, keepdims=True))
           - jnp.exp(jnp.sum(lq2_ref[...] * lk2_ref[...], axis=-1, keepdims=True)) + lam_init)
    o_ref[...] = (_rms(sm1 - lam * sm2, g_ref[...]) * (1.0 - lam_init)).astype(BF16)


def _diff_attention(lams, g, qb, kb, vb, *, lam_init):
    nq = DEC_SEQ // DIFF_TQ
    vec = _const_spec((1, DIFF_QK_DIM))
    qo = pl.BlockSpec((DIFF_TQ, LANES), lambda b, h, i: (b * nq + i, h))
    kv = pl.BlockSpec((DEC_SEQ, LANES), lambda b, h, i: (b, h))
    return pl.pallas_call(
        functools.partial(_diff_kernel, lam_init=lam_init),
        grid=(SEQ_BLOCKS, DIFF_HEADS, nq),
        in_specs=[vec, vec, vec, vec, _const_spec((1, DIFF_V_DIM)), qo, kv, kv],
        out_specs=qo,
        out_shape=jax.ShapeDtypeStruct((N_TOK, SECTION), BF16),
        scratch_shapes=[pltpu.VMEM((F32_SUBLANES, LANES), F32), pltpu.VMEM((2 * DIFF_TQ, LANES), F32),
                        pltpu.VMEM((2, 2 * DIFF_TQ, DIFF_TK), F32), pltpu.VMEM((2 * DIFF_TQ, 2 * LANES), F32)],
        compiler_params=_params("parallel", "parallel", "arbitrary"),
        name="diff_attention",
    )(*lams, g, qb, kb, vb)


def _outproj_kernel(x_ref, ya_ref, yb_ref, w_ref, o_ref):
    o_ref[...] = (x_ref[...]
                  + jnp.dot(ya_ref[...], w_ref[:SECTION, :], preferred_element_type=F32)
                  + jnp.dot(yb_ref[...], w_ref[SECTION:, :], preferred_element_type=F32))


def _outproj(x, ya, yb, w):
    tok = pl.BlockSpec((TM, D_MODEL), lambda i: (i, 0))
    sec = pl.BlockSpec((TM, SECTION), lambda i: (i, 0))
    return pl.pallas_call(
        _outproj_kernel,
        grid=(N_TOK // TM,),
        in_specs=[tok, sec, sec, _const_spec((2 * SECTION, D_MODEL))],
        out_specs=tok,
        out_shape=jax.ShapeDtypeStruct((N_TOK, D_MODEL), F32),
        compiler_params=_params("parallel"),
        name="outproj",
    )(x, ya, yb, w)


def _ffn_kernel(x_ref, xp_ref, xn_ref, g_ref, wup_ref, cw_ref, cb_ref, wdn_ref, gf_ref, o_ref, h_ref, *, final):
    i = pl.program_id(0)
    tiles_prompt = N_PROMPT // TM
    per_seq = jnp.where(i < tiles_prompt, SEQ // TM, DEC_SEQ // TM)
    has_prev = (i % per_seq) != 0
    has_next = (i % per_seq) != per_seq - 1

    x = x_ref[...]
    xe = jnp.concatenate([xp_ref[...], x, xn_ref[...]], axis=0)
    n = _rms(xe, g_ref[...]).astype(BF16)
    ext = TM + 2 * HALO
    row = lax.broadcasted_iota(jnp.int32, (ext, 1), 0)
    keep = ((row >= HALO) | has_prev) & ((row < HALO + TM) | has_next)

    for c in range(D_FF // FF_CHUNK):
        lo = c * FF_CHUNK
        gate = jnp.dot(n, wup_ref[:, lo:lo + FF_CHUNK], preferred_element_type=F32)
        val = jnp.dot(n, wup_ref[:, D_FF + lo:D_FF + lo + FF_CHUNK], preferred_element_type=F32)
        gate = jnp.where(keep, gate, 0.0)
        cw = cw_ref[:, lo:lo + FF_CHUNK]
        conv = (pltpu.roll(gate, 1, 0) * cw[0:1] + gate * cw[1:2] + pltpu.roll(gate, ext - 1, 0) * cw[2:3]
                + cb_ref[:, lo:lo + FF_CHUNK])
        act = 0.5 * conv * (1.0 + lax.erf(conv * (1.0 / math.sqrt(2.0))))
        h_ref[:, lo:lo + FF_CHUNK] = (act * val)[HALO:HALO + TM].astype(BF16)

    y = x + jnp.dot(h_ref[...], wdn_ref[...], preferred_element_type=F32)
    o_ref[...] = _rms(y, gf_ref[...]) if final else y


def _ffn(x, g, wup, cw, cb, wdn, gf, *, final):
    tok = pl.BlockSpec((TM, D_MODEL), lambda i: (i, 0))
    blocks_per_tile = TM // HALO
    prev = pl.BlockSpec((HALO, D_MODEL), lambda i: (jnp.maximum(i * blocks_per_tile - 1, 0), 0))
    nxt = pl.BlockSpec((HALO, D_MODEL), lambda i: (jnp.minimum((i + 1) * blocks_per_tile, N_TOK // HALO - 1), 0))
    return pl.pallas_call(
        functools.partial(_ffn_kernel, final=final),
        grid=(N_TOK // TM,),
        in_specs=[tok, prev, nxt, _const_spec((1, D_MODEL)), _const_spec((D_MODEL, 2 * D_FF)),
                  _const_spec((3, D_FF)), _const_spec((1, D_FF)), _const_spec((D_FF, D_MODEL)),
                  _const_spec((1, D_MODEL))],
        out_specs=tok,
        out_shape=jax.ShapeDtypeStruct((N_TOK, D_MODEL), F32),
        scratch_shapes=[pltpu.VMEM((TM, D_FF), BF16)],
        compiler_params=_params("parallel"),
        name="ffn_final" if final else "ffn",
    )(x, x, x, g, wup, cw, cb, wdn, gf)


def kernel(x_prompt, x_sample, g_attn, w_in, rpb, lam_q1, lam_k1, lam_q2, lam_k2, subln_g, w_out, g_ffn, w_up,
           conv_w, conv_b, w_down, g_final):
    x = jnp.concatenate([x_prompt.reshape(N_PROMPT, D_MODEL), x_sample.reshape(N_SAMPLE, D_MODEL)], axis=0)
    cos, sin_lo, sin_hi = _rotary_tables()
    row = lambda v: v.reshape(1, -1).astype(F32)
    for l in range(DEPTH):
        lam_init = 0.8 - 0.6 * math.exp(-0.3 * l)
        qa, ka, va, qb, kb, vb = _inproj(x, row(g_attn[l]), w_in[l].astype(BF16), cos, sin_lo, sin_hi)
        ya = _na_attention(qa, ka, va, _na_col_table(rpb[l]))
        lams = (row(lam_q1[l]), row(lam_k1[l]), row(lam_q2[l]), row(lam_k2[l]))
        yb = _diff_attention(lams, row(subln_g[l]), qb, kb, vb, lam_init=lam_init)
        x = _outproj(x, ya, yb, w_out[l].astype(BF16))
        x = _ffn(x, row(g_ffn[l]), w_up[l].astype(BF16), conv_w[l].astype(F32), row(conv_b[l]),
                 w_down[l].astype(BF16), row(g_final), final=(l == DEPTH - 1))
    return (x[:N_PROMPT].reshape(BATCH, SEQ, D_MODEL), x[N_PROMPT:].reshape(DEC_BATCH, DEC_SEQ, D_MODEL))
```

```python
import functools
import math

import jax
import jax.numpy as jnp
import numpy as np
from jax import lax
from jax.experimental import pallas as pl
from jax.experimental.pallas import tpu as pltpu

F32 = jnp.float32
BF16 = jnp.bfloat16

D_MODEL = 1024
BATCH, SEQ = 4, 4096
DEC_BATCH, DEC_SEQ = 2, 8192
DEPTH = 4
GRID_W = 64
WIN_H, WIN_W = 8, 16
NA_HEADS, NA_HEAD_DIM = 8, 64
DIFF_HEADS, DIFF_QK_DIM, DIFF_V_DIM = 4, 64, 128
SECTION = 512
IN_WIDTH = 6 * SECTION
D_FF = 2816
EPS = 1e-6

N_PROMPT = BATCH * SEQ
N_SAMPLE = DEC_BATCH * DEC_SEQ
N_TOK = N_PROMPT + N_SAMPLE
SEQ_BLOCKS = N_TOK // DEC_SEQ
PROMPT_BLOCKS = N_PROMPT // DEC_SEQ
assert DEC_SEQ == 2 * SEQ and N_PROMPT % DEC_SEQ == 0

LANES = 128
F32_SUBLANES = 8
VMEM_LIMIT_BYTES = 56 * 1024 * 1024

TM = 512
NA_QROWS = 4
NA_BROWS = 12
NA_Q = NA_QROWS * GRID_W
NA_K = NA_BROWS * GRID_W
NA_UNROLL = 4
NEG = -1e30
DIFF_TQ = 512
DIFF_TK = 1024
DIFF_SAFE_SCORE = 50.0
FF_CHUNK = 256
HALO = F32_SUBLANES

_NT = (((1,), (1,)), ((), ()))


def _const_spec(shape):
    return pl.BlockSpec(shape, lambda *_: (0,) * len(shape), pipeline_mode=pl.Buffered(1))


def _params(*semantics):
    return pltpu.CompilerParams(dimension_semantics=semantics, vmem_limit_bytes=VMEM_LIMIT_BYTES)


def _rms(x, g):
    return x * lax.rsqrt(jnp.mean(x * x, axis=-1, keepdims=True) + EPS) * g


def _split_heads(x):
    first = lax.broadcasted_iota(jnp.int32, (1, LANES), 1) < LANES // 2
    zero = jnp.zeros_like(x)
    return jnp.concatenate([jnp.where(first, x, zero), jnp.where(first, zero, x)], axis=0)


def _pos_block(i):
    tiles_prompt = N_PROMPT // TM
    return jnp.where(i < tiles_prompt, i % (SEQ // TM), (i - tiles_prompt) % (DEC_SEQ // TM))


def _inproj_kernel(x_ref, g_ref, w_ref, cos_ref, sin_lo_ref, sin_hi_ref,
                   qa_ref, ka_ref, va_ref, qb_ref, kb_ref, vb_ref):
    n = _rms(x_ref[...], g_ref[...]).astype(BF16)

    def proj(j):
        return jnp.dot(n, w_ref[:, j * SECTION:(j + 1) * SECTION], preferred_element_type=F32)

    qa_ref[...] = (proj(0) * (1.0 / math.sqrt(NA_HEAD_DIM))).astype(BF16)
    ka_ref[...] = proj(1).astype(BF16)
    va_ref[...] = proj(2).astype(BF16)
    cos, sin_lo, sin_hi = cos_ref[...], sin_lo_ref[...], sin_hi_ref[...]

    def rotary(x, scale):
        heads = []
        for h in range(DIFF_HEADS):
            xh = x[:, h * LANES:(h + 1) * LANES]
            r = xh * cos + pltpu.roll(xh, 96, 1) * sin_lo + pltpu.roll(xh, 32, 1) * sin_hi
            heads.append((r * scale).astype(BF16))
        return jnp.concatenate(heads, axis=1)

    qb_ref[...] = rotary(proj(3), 1.0 / math.sqrt(DIFF_QK_DIM))
    kb_ref[...] = rotary(proj(4), 1.0)
    vb_ref[...] = proj(5).astype(BF16)


def _inproj(x, g, w, cos, sin_lo, sin_hi):
    tok = pl.BlockSpec((TM, D_MODEL), lambda i: (i, 0))
    pos = pl.BlockSpec((TM, LANES), lambda i: (_pos_block(i), 0))
    sec = pl.BlockSpec((TM, SECTION), lambda i: (i, 0))
    return pl.pallas_call(
        _inproj_kernel,
        grid=(N_TOK // TM,),
        in_specs=[tok, _const_spec((1, D_MODEL)), _const_spec((D_MODEL, IN_WIDTH)), pos, pos, pos],
        out_specs=[sec] * 6,
        out_shape=[jax.ShapeDtypeStruct((N_TOK, SECTION), BF16)] * 6,
        compiler_params=_params("parallel"),
        name="inproj",
    )(x, g, w, cos, sin_lo, sin_hi)


def _rotary_tables():
    d = DIFF_QK_DIM
    inv_freq = 1.0 / (10000.0 ** (jnp.arange(0, d, 2, dtype=F32) / d))
    ang = jnp.arange(DEC_SEQ, dtype=F32)[:, None] * inv_freq[None, :]
    cos, sin, zero = jnp.cos(ang), jnp.sin(ang), jnp.zeros_like(ang)
    tile = lambda a, b: jnp.concatenate([a, b, a, b], axis=-1)
    return tile(cos, cos), tile(-sin, zero), tile(zero, sin)


def _na_row_patterns(rows):
    nblk = rows // NA_QROWS

    def one(j):
        r0 = j * NA_QROWS
        band = int(np.clip(r0 - WIN_H // 2, 0, rows - NA_BROWS))
        r = r0 + np.arange(NA_QROWS)
        start = np.clip(r - WIN_H // 2, 0, rows - WIN_H)
        kr = band + np.arange(NA_BROWS)
        ok = (kr[None, :] >= start[:, None]) & (kr[None, :] < start[:, None] + WIN_H)
        dr = kr[None, :] - r[:, None] + (WIN_H - 1)
        return ok, np.where(ok, dr, 0)

    pats = [one(0), one(1), one(nblk - 1)]
    for j in range(1, nblk - 1):
        assert all((a == b).all() for a, b in zip(one(j), pats[1]))
    return np.stack([p[0] for p in pats]), np.stack([p[1] for p in pats])


def _na_col_table(rpb_l):
    cols = np.arange(GRID_W)
    col_start = np.clip(cols - WIN_W // 2, 0, GRID_W - WIN_W)
    ok = (cols[None, :] >= col_start[:, None]) & (cols[None, :] < col_start[:, None] + WIN_W)
    dc = cols[None, :] - cols[:, None] + (WIN_W - 1)
    onehot = ((dc[None] == np.arange(2 * WIN_W - 1)[:, None, None]) & ok[None]).astype(np.float32)
    t = jnp.einsum("hrd,dck->hrck", rpb_l.astype(F32), jnp.asarray(onehot), precision=lax.Precision.HIGHEST)
    t = jnp.where(jnp.asarray(ok), t, NEG)
    return jnp.concatenate([t, t], axis=-1)


def _na_kernel(q_ref, k_ref, v_ref, tt_ref, o_ref, bias_ref, s_ref):
    blk = pl.program_id(1)

    @pl.when(blk == 0)
    def _():
        ok, dr = _na_row_patterns(SEQ // GRID_W)
        ok2, dr2 = _na_row_patterns(DEC_SEQ // GRID_W)
        assert (ok == ok2).all() and (dr == dr2).all()
        neg = jnp.full((GRID_W, GRID_W), NEG, F32)
        for pat in range(3):
            for hh in range(2):
                for qr in range(NA_QROWS):
                    r = hh * NA_Q + qr * GRID_W
                    for w in range(NA_BROWS):
                        half = (w % 2) * GRID_W
                        blockval = tt_ref[hh, int(dr[pat, qr, w]), :, half:half + GRID_W] if ok[pat, qr, w] else neg
                        bias_ref[pat, r:r + GRID_W, w * GRID_W:(w + 1) * GRID_W] = blockval

    is_prompt = blk < PROMPT_BLOCKS
    rows = jnp.where(is_prompt, SEQ // GRID_W, DEC_SEQ // GRID_W)
    nblk_p, nblk_s = SEQ // GRID_W // NA_QROWS, DEC_SEQ // GRID_W // NA_QROWS
    first = lax.broadcasted_iota(jnp.int32, (1, LANES), 1) < NA_HEAD_DIM

    def place(j):
        jr = jnp.where(is_prompt, j % nblk_p, j)
        base = jnp.where(is_prompt, (j // nblk_p) * (SEQ // GRID_W), 0)
        last = jnp.where(is_prompt, nblk_p, nblk_s) - 1
        r0 = jr * NA_QROWS
        band = jnp.clip(r0 - WIN_H // 2, 0, rows - NA_BROWS)
        pat = jnp.where(jr == 0, 0, jnp.where(jr == last, 2, 1))
        qoff = pl.multiple_of((base + r0) * GRID_W, NA_Q)
        koff = pl.multiple_of((base + band) * GRID_W, GRID_W)
        return qoff, koff, pat

    def scores(j, slot):
        qoff, koff, pat = place(j)
        qs = _split_heads(q_ref[pl.ds(qoff, NA_Q), :])
        s_ref[slot] = (lax.dot_general(qs, k_ref[pl.ds(koff, NA_K), :], _NT, preferred_element_type=F32)
                       + bias_ref[pat])

    def finish(j, slot):
        qoff, koff, _ = place(j)
        s = s_ref[slot]
        e = jnp.exp(s - jnp.max(s, axis=-1, keepdims=True))
        l = jnp.sum(e, axis=-1, keepdims=True)
        o = jnp.dot(e.astype(BF16), v_ref[pl.ds(koff, NA_K), :], preferred_element_type=F32) / l
        o_ref[pl.ds(qoff, NA_Q), :] = jnp.where(first, o[:NA_Q], o[NA_Q:]).astype(BF16)

    def body(g, carry):
        for u in range(NA_UNROLL):
            j = g * NA_UNROLL + u
            scores(jnp.minimum(j + 1, nblk_s - 1), (u + 1) % 2)
            finish(j, u % 2)
        return carry

    scores(0, 0)
    lax.fori_loop(0, nblk_s // NA_UNROLL, body, 0)


def _na_attention(qa, ka, va, tt):
    blk = pl.BlockSpec((DEC_SEQ, LANES), lambda hp, b: (b, hp))
    return pl.pallas_call(
        _na_kernel,
        grid=(NA_HEADS // 2, SEQ_BLOCKS),
        in_specs=[blk, blk, blk,
                  pl.BlockSpec((2, 2 * WIN_H - 1, GRID_W, LANES), lambda hp, b: (hp, 0, 0, 0))],
        out_specs=blk,
        out_shape=jax.ShapeDtypeStruct((N_TOK, SECTION), BF16),
        scratch_shapes=[pltpu.VMEM((3, 2 * NA_Q, NA_K), F32), pltpu.VMEM((2, 2 * NA_Q, NA_K), F32)],
        compiler_params=_params("arbitrary", "arbitrary"),
        name="na_attention",
    )(qa, ka, va, tt)


def _half_sums(x):
    first = lax.broadcasted_iota(jnp.int32, (1, LANES), 1) < LANES // 2
    zero = jnp.zeros_like(x)
    return (jnp.sum(jnp.where(first, x, zero), axis=-1, keepdims=True),
            jnp.sum(jnp.where(first, zero, x), axis=-1, keepdims=True))


def _diff_kernel(lq1_ref, lk1_ref, lq2_ref, lk2_ref, g_ref, q_ref, k_ref, v_ref, o_ref, s_ref, acc_ref, m_ref,
                 *, lam_init):
    is_prompt = pl.program_id(0) < PROMPT_BLOCKS
    n_qblocks = DEC_SEQ // DIFF_TQ
    ones = jnp.ones((DIFF_TK, LANES), BF16)

    def key_rows(qi, c, seq):
        base = (qi // (seq // DIFF_TQ)) * seq
        return pl.ds(pl.multiple_of(base + c * DIFF_TK, DIFF_TK), DIFF_TK)

    def scores(qi, c, slot, seq):
        q = q_ref[pl.ds(pl.multiple_of(qi * DIFF_TQ, DIFF_TQ), DIFF_TQ), :]
        s_ref[slot] = lax.dot_general(_split_heads(q), k_ref[key_rows(qi, c, seq), :], _NT,
                                      preferred_element_type=F32)

    def accumulate(qi, c, slot, seq, shifted):
        tiles = []
        for j in range(DIFF_TK // LANES):
            s = s_ref[slot, :, j * LANES:(j + 1) * LANES]
            tiles.append(jnp.exp(s - m_ref[...] if shifted else s))
        p = jnp.concatenate(tiles, axis=1).astype(BF16)
        acc_ref[...] += jnp.dot(p, jnp.concatenate([v_ref[key_rows(qi, c, seq), :], ones], axis=1),
                                preferred_element_type=F32)

    def finish(qi):
        acc = acc_ref[...]
        sm1 = acc[:DIFF_TQ, :LANES] / acc[:DIFF_TQ, LANES:]
        sm2 = acc[DIFF_TQ:, :LANES] / acc[DIFF_TQ:, LANES:]
        lam = (jnp.exp(jnp.sum(lq1_ref[...] * lk1_ref[...], axis=-1, keepdims=True))
               - jnp.exp(jnp.sum(lq2_ref[...] * lk2_ref[...], axis=-1, keepdims=True)) + lam_init)
        y = _rms(sm1 - lam * sm2, g_ref[...]) * (1.0 - lam_init)
        o_ref[pl.ds(pl.multiple_of(qi * DIFF_TQ, DIFF_TQ), DIFF_TQ), :] = y.astype(BF16)
        acc_ref[...] = jnp.zeros_like(acc_ref)

    def max_sq_norms(ref):
        def body(c, carry):
            x = ref[pl.ds(pl.multiple_of(c * DIFF_TQ, DIFF_TQ), DIFF_TQ), :].astype(F32)
            n1, n2 = _half_sums(x * x)
            return (jnp.maximum(carry[0], jnp.max(n1, axis=0, keepdims=True)),
                    jnp.maximum(carry[1], jnp.max(n2, axis=0, keepdims=True)))
        zero = jnp.zeros((1, 1), F32)
        return lax.fori_loop(0, n_qblocks, body, (zero, zero))

    (q1, q2), (k1, k2) = max_sq_norms(q_ref), max_sq_norms(k_ref)
    needs_max = jnp.max(jnp.maximum(q1 * k1, q2 * k2)) > DIFF_SAFE_SCORE ** 2
    acc_ref[...] = jnp.zeros_like(acc_ref)

    def unshifted(seq):
        nchunks = seq // DIFF_TK
        assert nchunks % 2 == 0

        def body(qi, carry):
            for c in range(nchunks):
                if c + 1 < nchunks:
                    scores(qi, c + 1, (c + 1) % 2, seq)
                else:
                    scores(jnp.minimum(qi + 1, n_qblocks - 1), 0, 0, seq)
                accumulate(qi, c, c % 2, seq, False)
            finish(qi)
            return carry

        def run():
            scores(0, 0, 0, seq)
            lax.fori_loop(0, n_qblocks, body, 0)
        return run

    def shifted(seq):
        nchunks = seq // DIFF_TK

        def qblock(qi, carry):
            def row_max(c, m):
                scores(qi, c, 0, seq)
                return jnp.maximum(m, jnp.max(s_ref[0], axis=-1, keepdims=True))

            m = lax.fori_loop(0, nchunks, row_max, jnp.full((2 * DIFF_TQ, 1), NEG, F32))
            m_ref[...] = jnp.broadcast_to(m, m_ref.shape)

            def chunk(c, carry):
                scores(qi, c, 0, seq)
                accumulate(qi, c, 0, seq, True)
                return carry

            lax.fori_loop(0, nchunks, chunk, 0)
            finish(qi)
            return carry

        def run():
            lax.fori_loop(0, n_qblocks, qblock, 0)
        return run

    lax.cond(needs_max,
             lambda: lax.cond(is_prompt, shifted(SEQ), shifted(DEC_SEQ)),
             lambda: lax.cond(is_prompt, unshifted(SEQ), unshifted(DEC_SEQ)))


def _diff_attention(lams, g, qb, kb, vb, *, lam_init):
    vec = _const_spec((1, DIFF_QK_DIM))
    blk = pl.BlockSpec((DEC_SEQ, LANES), lambda b, h: (b, h))
    return pl.pallas_call(
        functools.partial(_diff_kernel, lam_init=lam_init),
        grid=(SEQ_BLOCKS, DIFF_HEADS),
        in_specs=[vec, vec, vec, vec, _const_spec((1, DIFF_V_DIM)), blk, blk, blk],
        out_specs=blk,
        out_shape=jax.ShapeDtypeStruct((N_TOK, SECTION), BF16),
        scratch_shapes=[pltpu.VMEM((2, 2 * DIFF_TQ, DIFF_TK), F32), pltpu.VMEM((2 * DIFF_TQ, 2 * LANES), F32),
                        pltpu.VMEM((2 * DIFF_TQ, LANES), F32)],
        compiler_params=_params("parallel", "parallel"),
        name="diff_attention",
    )(*lams, g, qb, kb, vb)


def _outproj_kernel(x_ref, ya_ref, yb_ref, w_ref, o_ref):
    o_ref[...] = (x_ref[...]
                  + jnp.dot(ya_ref[...], w_ref[:SECTION, :], preferred_element_type=F32)
                  + jnp.dot(yb_ref[...], w_ref[SECTION:, :], preferred_element_type=F32))


def _outproj(x, ya, yb, w):
    tok = pl.BlockSpec((TM, D_MODEL), lambda i: (i, 0))
    sec = pl.BlockSpec((TM, SECTION), lambda i: (i, 0))
    return pl.pallas_call(
        _outproj_kernel,
        grid=(N_TOK // TM,),
        in_specs=[tok, sec, sec, _const_spec((2 * SECTION, D_MODEL))],
        out_specs=tok,
        out_shape=jax.ShapeDtypeStruct((N_TOK, D_MODEL), F32),
        compiler_params=_params("parallel"),
        name="outproj",
    )(x, ya, yb, w)


def _ffn_kernel(x_ref, xp_ref, xn_ref, g_ref, wup_ref, cw_ref, cb_ref, wdn_ref, gf_ref, o_ref, h_ref, *, final):
    i = pl.program_id(0)
    tiles_prompt = N_PROMPT // TM
    per_seq = jnp.where(i < tiles_prompt, SEQ // TM, DEC_SEQ // TM)
    has_prev = (i % per_seq) != 0
    has_next = (i % per_seq) != per_seq - 1

    x = x_ref[...]
    xe = jnp.concatenate([xp_ref[...], x, xn_ref[...]], axis=0)
    n = _rms(xe, g_ref[...]).astype(BF16)
    ext = TM + 2 * HALO
    row = lax.broadcasted_iota(jnp.int32, (ext, 1), 0)
    keep = ((row >= HALO) | has_prev) & ((row < HALO + TM) | has_next)

    for c in range(D_FF // FF_CHUNK):
        lo = c * FF_CHUNK
        gate = jnp.dot(n, wup_ref[:, lo:lo + FF_CHUNK], preferred_element_type=F32)
        val = jnp.dot(n, wup_ref[:, D_FF + lo:D_FF + lo + FF_CHUNK], preferred_element_type=F32)
        gate = jnp.where(keep, gate, 0.0)
        cw = cw_ref[:, lo:lo + FF_CHUNK]
        conv = (pltpu.roll(gate, 1, 0) * cw[0:1] + gate * cw[1:2] + pltpu.roll(gate, ext - 1, 0) * cw[2:3]
                + cb_ref[:, lo:lo + FF_CHUNK])
        act = 0.5 * conv * (1.0 + lax.erf(conv * (1.0 / math.sqrt(2.0))))
        h_ref[:, lo:lo + FF_CHUNK] = (act * val)[HALO:HALO + TM].astype(BF16)

    y = x + jnp.dot(h_ref[...], wdn_ref[...], preferred_element_type=F32)
    o_ref[...] = _rms(y, gf_ref[...]) if final else y


def _ffn(x, g, wup, cw, cb, wdn, gf, *, final):
    tok = pl.BlockSpec((TM, D_MODEL), lambda i: (i, 0))
    blocks_per_tile = TM // HALO
    prev = pl.BlockSpec((HALO, D_MODEL), lambda i: (jnp.maximum(i * blocks_per_tile - 1, 0), 0))
    nxt = pl.BlockSpec((HALO, D_MODEL), lambda i: (jnp.minimum((i + 1) * blocks_per_tile, N_TOK // HALO - 1), 0))
    return pl.pallas_call(
        functools.partial(_ffn_kernel, final=final),
        grid=(N_TOK // TM,),
        in_specs=[tok, prev, nxt, _const_spec((1, D_MODEL)), _const_spec((D_MODEL, 2 * D_FF)),
                  _const_spec((3, D_FF)), _const_spec((1, D_FF)), _const_spec((D_FF, D_MODEL)),
                  _const_spec((1, D_MODEL))],
        out_specs=tok,
        out_shape=jax.ShapeDtypeStruct((N_TOK, D_MODEL), F32),
        scratch_shapes=[pltpu.VMEM((TM, D_FF), BF16)],
        compiler_params=_params("parallel"),
        name="ffn_final" if final else "ffn",
    )(x, x, x, g, wup, cw, cb, wdn, gf)


def kernel(x_prompt, x_sample, g_attn, w_in, rpb, lam_q1, lam_k1, lam_q2, lam_k2, subln_g, w_out, g_ffn, w_up,
           conv_w, conv_b, w_down, g_final):
    x = jnp.concatenate([x_prompt.reshape(N_PROMPT, D_MODEL), x_sample.reshape(N_SAMPLE, D_MODEL)], axis=0)
    cos, sin_lo, sin_hi = _rotary_tables()
    row = lambda v: v.reshape(1, -1).astype(F32)
    for l in range(DEPTH):
        lam_init = 0.8 - 0.6 * math.exp(-0.3 * l)
        qa, ka, va, qb, kb, vb = _inproj(x, row(g_attn[l]), w_in[l].astype(BF16), cos, sin_lo, sin_hi)
        ya = _na_attention(qa, ka, va, _na_col_table(rpb[l]))
        lams = (row(lam_q1[l]), row(lam_k1[l]), row(lam_q2[l]), row(lam_k2[l]))
        yb = _diff_attention(lams, row(subln_g[l]), qb, kb, vb, lam_init=lam_init)
        x = _outproj(x, ya, yb, w_out[l].astype(BF16))
        x = _ffn(x, row(g_ffn[l]), w_up[l].astype(BF16), conv_w[l].astype(F32), row(conv_b[l]),
                 w_down[l].astype(BF16), row(g_final), final=(l == DEPTH - 1))
    return (x[:N_PROMPT].reshape(BATCH, SEQ, D_MODEL), x[N_PROMPT:].reshape(DEC_BATCH, DEC_SEQ, D_MODEL))
```

```python
import functools
import math

import jax
import jax.numpy as jnp
import numpy as np
from jax import lax
from jax.experimental import pallas as pl
from jax.experimental.pallas import tpu as pltpu

F32 = jnp.float32
BF16 = jnp.bfloat16

D_MODEL = 1024
BATCH, SEQ = 4, 4096
DEC_BATCH, DEC_SEQ = 2, 8192
DEPTH = 4
GRID_W = 64
WIN_H, WIN_W = 8, 16
NA_HEADS, NA_HEAD_DIM = 8, 64
DIFF_HEADS, DIFF_QK_DIM, DIFF_V_DIM = 4, 64, 128
SECTION = 512
IN_WIDTH = 6 * SECTION
D_FF = 2816
EPS = 1e-6

N_PROMPT = BATCH * SEQ
N_SAMPLE = DEC_BATCH * DEC_SEQ
N_TOK = N_PROMPT + N_SAMPLE
SEQ_BLOCKS = N_TOK // DEC_SEQ
PROMPT_BLOCKS = N_PROMPT // DEC_SEQ
assert DEC_SEQ == 2 * SEQ and N_PROMPT % DEC_SEQ == 0

LANES = 128
F32_SUBLANES = 8
VMEM_LIMIT_BYTES = 56 * 1024 * 1024

TM = 512
NA_QROWS = 4
NA_BROWS = 12
NA_Q = NA_QROWS * GRID_W
NA_K = NA_BROWS * GRID_W
NA_UNROLL = 4
NEG = -1e30
DIFF_TQ = 512
DIFF_TK = 1024
DIFF_VT_ROWS = DIFF_V_DIM + 16
DIFF_SAFE_SCORE = 50.0
FF_CHUNK = 256
HALO = F32_SUBLANES

_NT = (((1,), (1,)), ((), ()))


def _const_spec(shape):
    return pl.BlockSpec(shape, lambda *_: (0,) * len(shape), pipeline_mode=pl.Buffered(1))


def _params(*semantics):
    return pltpu.CompilerParams(dimension_semantics=semantics, vmem_limit_bytes=VMEM_LIMIT_BYTES)


def _rms(x, g):
    return x * lax.rsqrt(jnp.mean(x * x, axis=-1, keepdims=True) + EPS) * g


def _split_heads(x):
    first = lax.broadcasted_iota(jnp.int32, (1, LANES), 1) < LANES // 2
    zero = jnp.zeros_like(x)
    return jnp.concatenate([jnp.where(first, x, zero), jnp.where(first, zero, x)], axis=0)


def _pos_block(i):
    tiles_prompt = N_PROMPT // TM
    return jnp.where(i < tiles_prompt, i % (SEQ // TM), (i - tiles_prompt) % (DEC_SEQ // TM))


def _inproj_kernel(x_ref, g_ref, w_ref, cos_ref, sin_lo_ref, sin_hi_ref,
                   qa_ref, ka_ref, va_ref, qb_ref, kb_ref, vb_ref):
    n = _rms(x_ref[...], g_ref[...]).astype(BF16)

    def proj(j):
        return jnp.dot(n, w_ref[:, j * SECTION:(j + 1) * SECTION], preferred_element_type=F32)

    qa_ref[...] = (proj(0) * (1.0 / math.sqrt(NA_HEAD_DIM))).astype(BF16)
    ka_ref[...] = proj(1).astype(BF16)
    va_ref[...] = proj(2).astype(BF16)
    cos, sin_lo, sin_hi = cos_ref[...], sin_lo_ref[...], sin_hi_ref[...]

    def rotary(x, scale):
        heads = []
        for h in range(DIFF_HEADS):
            xh = x[:, h * LANES:(h + 1) * LANES]
            r = xh * cos + pltpu.roll(xh, 96, 1) * sin_lo + pltpu.roll(xh, 32, 1) * sin_hi
            heads.append((r * scale).astype(BF16))
        return jnp.concatenate(heads, axis=1)

    qb_ref[...] = rotary(proj(3), 1.0 / math.sqrt(DIFF_QK_DIM))
    kb_ref[...] = rotary(proj(4), 1.0)
    vb_ref[...] = proj(5).astype(BF16)


def _inproj(x, g, w, cos, sin_lo, sin_hi):
    tok = pl.BlockSpec((TM, D_MODEL), lambda i: (i, 0))
    pos = pl.BlockSpec((TM, LANES), lambda i: (_pos_block(i), 0))
    sec = pl.BlockSpec((TM, SECTION), lambda i: (i, 0))
    return pl.pallas_call(
        _inproj_kernel,
        grid=(N_TOK // TM,),
        in_specs=[tok, _const_spec((1, D_MODEL)), _const_spec((D_MODEL, IN_WIDTH)), pos, pos, pos],
        out_specs=[sec] * 6,
        out_shape=[jax.ShapeDtypeStruct((N_TOK, SECTION), BF16)] * 6,
        compiler_params=_params("parallel"),
        name="inproj",
    )(x, g, w, cos, sin_lo, sin_hi)


def _rotary_tables():
    d = DIFF_QK_DIM
    inv_freq = 1.0 / (10000.0 ** (jnp.arange(0, d, 2, dtype=F32) / d))
    ang = jnp.arange(DEC_SEQ, dtype=F32)[:, None] * inv_freq[None, :]
    cos, sin, zero = jnp.cos(ang), jnp.sin(ang), jnp.zeros_like(ang)
    tile = lambda a, b: jnp.concatenate([a, b, a, b], axis=-1)
    return tile(cos, cos), tile(-sin, zero), tile(zero, sin)


def _na_row_patterns(rows):
    nblk = rows // NA_QROWS

    def one(j):
        r0 = j * NA_QROWS
        band = int(np.clip(r0 - WIN_H // 2, 0, rows - NA_BROWS))
        r = r0 + np.arange(NA_QROWS)
        start = np.clip(r - WIN_H // 2, 0, rows - WIN_H)
        kr = band + np.arange(NA_BROWS)
        ok = (kr[None, :] >= start[:, None]) & (kr[None, :] < start[:, None] + WIN_H)
        dr = kr[None, :] - r[:, None] + (WIN_H - 1)
        return ok, np.where(ok, dr, 0)

    pats = [one(0), one(1), one(nblk - 1)]
    for j in range(1, nblk - 1):
        assert all((a == b).all() for a, b in zip(one(j), pats[1]))
    return np.stack([p[0] for p in pats]), np.stack([p[1] for p in pats])


def _na_col_table(rpb_l):
    cols = np.arange(GRID_W)
    col_start = np.clip(cols - WIN_W // 2, 0, GRID_W - WIN_W)
    ok = (cols[None, :] >= col_start[:, None]) & (cols[None, :] < col_start[:, None] + WIN_W)
    dc = cols[None, :] - cols[:, None] + (WIN_W - 1)
    onehot = ((dc[None] == np.arange(2 * WIN_W - 1)[:, None, None]) & ok[None]).astype(np.float32)
    t = jnp.einsum("hrd,dck->hrck", rpb_l.astype(F32), jnp.asarray(onehot), precision=lax.Precision.HIGHEST)
    t = jnp.where(jnp.asarray(ok), t, NEG)
    return jnp.concatenate([t, t], axis=-1)


def _na_kernel(q_ref, k_ref, v_ref, tt_ref, o_ref, bias_ref, s_ref):
    blk = pl.program_id(1)

    @pl.when(blk == 0)
    def _():
        ok, dr = _na_row_patterns(SEQ // GRID_W)
        ok2, dr2 = _na_row_patterns(DEC_SEQ // GRID_W)
        assert (ok == ok2).all() and (dr == dr2).all()
        neg = jnp.full((GRID_W, GRID_W), NEG, F32)
        for pat in range(3):
            for hh in range(2):
                for qr in range(NA_QROWS):
                    r = hh * NA_Q + qr * GRID_W
                    for w in range(NA_BROWS):
                        half = (w % 2) * GRID_W
                        blockval = tt_ref[hh, int(dr[pat, qr, w]), :, half:half + GRID_W] if ok[pat, qr, w] else neg
                        bias_ref[pat, r:r + GRID_W, w * GRID_W:(w + 1) * GRID_W] = blockval

    is_prompt = blk < PROMPT_BLOCKS
    rows = jnp.where(is_prompt, SEQ // GRID_W, DEC_SEQ // GRID_W)
    nblk_p, nblk_s = SEQ // GRID_W // NA_QROWS, DEC_SEQ // GRID_W // NA_QROWS
    first = lax.broadcasted_iota(jnp.int32, (1, LANES), 1) < NA_HEAD_DIM

    def place(j):
        jr = jnp.where(is_prompt, j % nblk_p, j)
        base = jnp.where(is_prompt, (j // nblk_p) * (SEQ // GRID_W), 0)
        last = jnp.where(is_prompt, nblk_p, nblk_s) - 1
        r0 = jr * NA_QROWS
        band = jnp.clip(r0 - WIN_H // 2, 0, rows - NA_BROWS)
        pat = jnp.where(jr == 0, 0, jnp.where(jr == last, 2, 1))
        qoff = pl.multiple_of((base + r0) * GRID_W, NA_Q)
        koff = pl.multiple_of((base + band) * GRID_W, GRID_W)
        return qoff, koff, pat

    def scores(j, slot):
        qoff, koff, pat = place(j)
        qs = _split_heads(q_ref[pl.ds(qoff, NA_Q), :])
        s_ref[slot] = (lax.dot_general(qs, k_ref[pl.ds(koff, NA_K), :], _NT, preferred_element_type=F32)
                       + bias_ref[pat])

    def finish(j, slot):
        qoff, koff, _ = place(j)
        s = s_ref[slot]
        e = jnp.exp(s - jnp.max(s, axis=-1, keepdims=True))
        l = jnp.sum(e, axis=-1, keepdims=True)
        o = jnp.dot(e.astype(BF16), v_ref[pl.ds(koff, NA_K), :], preferred_element_type=F32) / l
        o_ref[pl.ds(qoff, NA_Q), :] = jnp.where(first, o[:NA_Q], o[NA_Q:]).astype(BF16)

    def body(g, carry):
        for u in range(NA_UNROLL):
            j = g * NA_UNROLL + u
            scores(jnp.minimum(j + 1, nblk_s - 1), (u + 1) % 2)
            finish(j, u % 2)
        return carry

    scores(0, 0)
    lax.fori_loop(0, nblk_s // NA_UNROLL, body, 0)


def _na_attention(qa, ka, va, tt):
    blk = pl.BlockSpec((DEC_SEQ, LANES), lambda hp, b: (b, hp))
    return pl.pallas_call(
        _na_kernel,
        grid=(NA_HEADS // 2, SEQ_BLOCKS),
        in_specs=[blk, blk, blk,
                  pl.BlockSpec((2, 2 * WIN_H - 1, GRID_W, LANES), lambda hp, b: (hp, 0, 0, 0))],
        out_specs=blk,
        out_shape=jax.ShapeDtypeStruct((N_TOK, SECTION), BF16),
        scratch_shapes=[pltpu.VMEM((3, 2 * NA_Q, NA_K), F32), pltpu.VMEM((2, 2 * NA_Q, NA_K), F32)],
        compiler_params=_params("arbitrary", "arbitrary"),
        name="na_attention",
    )(qa, ka, va, tt)


def _half_sums(x):
    first = lax.broadcasted_iota(jnp.int32, (1, LANES), 1) < LANES // 2
    zero = jnp.zeros_like(x)
    return (jnp.sum(jnp.where(first, x, zero), axis=-1, keepdims=True),
            jnp.sum(jnp.where(first, zero, x), axis=-1, keepdims=True))


def _diff_kernel(lq1_ref, lk1_ref, lq2_ref, lk2_ref, g_ref, q_ref, k_ref, v_ref, o_ref, s_ref, acc_ref, m_ref,
                 vt_ref, *, lam_init):
    is_prompt = pl.program_id(0) < PROMPT_BLOCKS
    n_qblocks = DEC_SEQ // DIFF_TQ

    def transpose_values(c, carry):
        rows = pl.ds(pl.multiple_of(c * DIFF_TQ, DIFF_TQ), DIFF_TQ)
        vt_ref[0:DIFF_V_DIM, rows] = v_ref[rows, :].astype(F32).T.astype(BF16)
        return carry

    lax.fori_loop(0, n_qblocks, transpose_values, 0)
    vt_ref[DIFF_V_DIM:, :] = jnp.ones((DIFF_VT_ROWS - DIFF_V_DIM, DEC_SEQ), BF16)

    def key_rows(qi, c, seq):
        base = (qi // (seq // DIFF_TQ)) * seq
        return pl.ds(pl.multiple_of(base + c * DIFF_TK, DIFF_TK), DIFF_TK)

    def scores(qi, c, slot, seq):
        q = q_ref[pl.ds(pl.multiple_of(qi * DIFF_TQ, DIFF_TQ), DIFF_TQ), :]
        s_ref[slot] = lax.dot_general(k_ref[key_rows(qi, c, seq), :], _split_heads(q), _NT,
                                      preferred_element_type=F32)

    def accumulate(qi, c, slot, seq, shifted):
        s = s_ref[slot]
        p = jnp.exp(s - m_ref[0:1, :] if shifted else s).astype(BF16)
        acc_ref[...] += jnp.dot(vt_ref[:, key_rows(qi, c, seq)], p, preferred_element_type=F32)

    def finish(qi):
        acc = acc_ref[...]
        sm1 = acc[:DIFF_V_DIM, :DIFF_TQ] / acc[DIFF_V_DIM:DIFF_V_DIM + 1, :DIFF_TQ]
        sm2 = acc[:DIFF_V_DIM, DIFF_TQ:] / acc[DIFF_V_DIM:DIFF_V_DIM + 1, DIFF_TQ:]
        lam = (jnp.exp(jnp.sum(lq1_ref[...] * lk1_ref[...], axis=-1, keepdims=True))
               - jnp.exp(jnp.sum(lq2_ref[...] * lk2_ref[...], axis=-1, keepdims=True)) + lam_init)
        d = sm1 - lam * sm2
        d = d * lax.rsqrt(jnp.mean(d * d, axis=0, keepdims=True) + EPS)
        y = d.T * g_ref[...] * (1.0 - lam_init)
        o_ref[pl.ds(pl.multiple_of(qi * DIFF_TQ, DIFF_TQ), DIFF_TQ), :] = y.astype(BF16)
        acc_ref[...] = jnp.zeros_like(acc_ref)

    def max_sq_norms(ref):
        def body(c, carry):
            x = ref[pl.ds(pl.multiple_of(c * DIFF_TQ, DIFF_TQ), DIFF_TQ), :].astype(F32)
            n1, n2 = _half_sums(x * x)
            return (jnp.maximum(carry[0], jnp.max(n1, axis=0, keepdims=True)),
                    jnp.maximum(carry[1], jnp.max(n2, axis=0, keepdims=True)))
        zero = jnp.zeros((1, 1), F32)
        return lax.fori_loop(0, n_qblocks, body, (zero, zero))

    (q1, q2), (k1, k2) = max_sq_norms(q_ref), max_sq_norms(k_ref)
    needs_max = jnp.max(jnp.maximum(q1 * k1, q2 * k2)) > DIFF_SAFE_SCORE ** 2
    acc_ref[...] = jnp.zeros_like(acc_ref)

    def unshifted(seq):
        nchunks = seq // DIFF_TK
        assert nchunks % 2 == 0

        def body(qi, carry):
            for c in range(nchunks):
                if c + 1 < nchunks:
                    scores(qi, c + 1, (c + 1) % 2, seq)
                else:
                    scores(jnp.minimum(qi + 1, n_qblocks - 1), 0, 0, seq)
                accumulate(qi, c, c % 2, seq, False)
            finish(qi)
            return carry

        def run():
            scores(0, 0, 0, seq)
            lax.fori_loop(0, n_qblocks, body, 0)
        return run

    def shifted(seq):
        nchunks = seq // DIFF_TK

        def qblock(qi, carry):
            def col_max(c, m):
                scores(qi, c, 0, seq)
                return jnp.maximum(m, jnp.max(s_ref[0], axis=0, keepdims=True))

            m = lax.fori_loop(0, nchunks, col_max, jnp.full((1, 2 * DIFF_TQ), NEG, F32))
            m_ref[...] = jnp.broadcast_to(m, m_ref.shape)

            def chunk(c, carry):
                scores(qi, c, 0, seq)
                accumulate(qi, c, 0, seq, True)
                return carry

            lax.fori_loop(0, nchunks, chunk, 0)
            finish(qi)
            return carry

        def run():
            lax.fori_loop(0, n_qblocks, qblock, 0)
        return run

    lax.cond(needs_max,
             lambda: lax.cond(is_prompt, shifted(SEQ), shifted(DEC_SEQ)),
             lambda: lax.cond(is_prompt, unshifted(SEQ), unshifted(DEC_SEQ)))


def _diff_attention(lams, g, qb, kb, vb, *, lam_init):
    vec = _const_spec((1, DIFF_QK_DIM))
    blk = pl.BlockSpec((DEC_SEQ, LANES), lambda b, h: (b, h))
    return pl.pallas_call(
        functools.partial(_diff_kernel, lam_init=lam_init),
        grid=(SEQ_BLOCKS, DIFF_HEADS),
        in_specs=[vec, vec, vec, vec, _const_spec((1, DIFF_V_DIM)), blk, blk, blk],
        out_specs=blk,
        out_shape=jax.ShapeDtypeStruct((N_TOK, SECTION), BF16),
        scratch_shapes=[pltpu.VMEM((2, DIFF_TK, 2 * DIFF_TQ), F32), pltpu.VMEM((DIFF_VT_ROWS, 2 * DIFF_TQ), F32),
                        pltpu.VMEM((F32_SUBLANES, 2 * DIFF_TQ), F32), pltpu.VMEM((DIFF_VT_ROWS, DEC_SEQ), BF16)],
        compiler_params=_params("parallel", "parallel"),
        name="diff_attention",
    )(*lams, g, qb, kb, vb)


def _outproj_kernel(x_ref, ya_ref, yb_ref, w_ref, o_ref):
    o_ref[...] = (x_ref[...]
                  + jnp.dot(ya_ref[...], w_ref[:SECTION, :], preferred_element_type=F32)
                  + jnp.dot(yb_ref[...], w_ref[SECTION:, :], preferred_element_type=F32))


def _outproj(x, ya, yb, w):
    tok = pl.BlockSpec((TM, D_MODEL), lambda i: (i, 0))
    sec = pl.BlockSpec((TM, SECTION), lambda i: (i, 0))
    return pl.pallas_call(
        _outproj_kernel,
        grid=(N_TOK // TM,),
        in_specs=[tok, sec, sec, _const_spec((2 * SECTION, D_MODEL))],
        out_specs=tok,
        out_shape=jax.ShapeDtypeStruct((N_TOK, D_MODEL), F32),
        compiler_params=_params("parallel"),
        name="outproj",
    )(x, ya, yb, w)


def _ffn_kernel(x_ref, xp_ref, xn_ref, g_ref, wup_ref, cw_ref, cb_ref, wdn_ref, gf_ref, o_ref, h_ref, *, final):
    i = pl.program_id(0)
    tiles_prompt = N_PROMPT // TM
    per_seq = jnp.where(i < tiles_prompt, SEQ // TM, DEC_SEQ // TM)
    has_prev = (i % per_seq) != 0
    has_next = (i % per_seq) != per_seq - 1

    x = x_ref[...]
    xe = jnp.concatenate([xp_ref[...], x, xn_ref[...]], axis=0)
    n = _rms(xe, g_ref[...]).astype(BF16)
    ext = TM + 2 * HALO
    row = lax.broadcasted_iota(jnp.int32, (ext, 1), 0)
    keep = ((row >= HALO) | has_prev) & ((row < HALO + TM) | has_next)

    for c in range(D_FF // FF_CHUNK):
        lo = c * FF_CHUNK
        gate = jnp.dot(n, wup_ref[:, lo:lo + FF_CHUNK], preferred_element_type=F32)
        val = jnp.dot(n, wup_ref[:, D_FF + lo:D_FF + lo + FF_CHUNK], preferred_element_type=F32)
        gate = jnp.where(keep, gate, 0.0)
        cw = cw_ref[:, lo:lo + FF_CHUNK]
        conv = (pltpu.roll(gate, 1, 0) * cw[0:1] + gate * cw[1:2] + pltpu.roll(gate, ext - 1, 0) * cw[2:3]
                + cb_ref[:, lo:lo + FF_CHUNK])
        act = 0.5 * conv * (1.0 + lax.erf(conv * (1.0 / math.sqrt(2.0))))
        h_ref[:, lo:lo + FF_CHUNK] = (act * val)[HALO:HALO + TM].astype(BF16)

    y = x + jnp.dot(h_ref[...], wdn_ref[...], preferred_element_type=F32)
    o_ref[...] = _rms(y, gf_ref[...]) if final else y


def _ffn(x, g, wup, cw, cb, wdn, gf, *, final):
    tok = pl.BlockSpec((TM, D_MODEL), lambda i: (i, 0))
    blocks_per_tile = TM // HALO
    prev = pl.BlockSpec((HALO, D_MODEL), lambda i: (jnp.maximum(i * blocks_per_tile - 1, 0), 0))
    nxt = pl.BlockSpec((HALO, D_MODEL), lambda i: (jnp.minimum((i + 1) * blocks_per_tile, N_TOK // HALO - 1), 0))
    return pl.pallas_call(
        functools.partial(_ffn_kernel, final=final),
        grid=(N_TOK // TM,),
        in_specs=[tok, prev, nxt, _const_spec((1, D_MODEL)), _const_spec((D_MODEL, 2 * D_FF)),
                  _const_spec((3, D_FF)), _const_spec((1, D_FF)), _const_spec((D_FF, D_MODEL)),
                  _const_spec((1, D_MODEL))],
        out_specs=tok,
        out_shape=jax.ShapeDtypeStruct((N_TOK, D_MODEL), F32),
        scratch_shapes=[pltpu.VMEM((TM, D_FF), BF16)],
        compiler_params=_params("parallel"),
        name="ffn_final" if final else "ffn",
    )(x, x, x, g, wup, cw, cb, wdn, gf)


def kernel(x_prompt, x_sample, g_attn, w_in, rpb, lam_q1, lam_k1, lam_q2, lam_k2, subln_g, w_out, g_ffn, w_up,
           conv_w, conv_b, w_down, g_final):
    x = jnp.concatenate([x_prompt.reshape(N_PROMPT, D_MODEL), x_sample.reshape(N_SAMPLE, D_MODEL)], axis=0)
    cos, sin_lo, sin_hi = _rotary_tables()
    row = lambda v: v.reshape(1, -1).astype(F32)
    for l in range(DEPTH):
        lam_init = 0.8 - 0.6 * math.exp(-0.3 * l)
        qa, ka, va, qb, kb, vb = _inproj(x, row(g_attn[l]), w_in[l].astype(BF16), cos, sin_lo, sin_hi)
        ya = _na_attention(qa, ka, va, _na_col_table(rpb[l]))
        lams = (row(lam_q1[l]), row(lam_k1[l]), row(lam_q2[l]), row(lam_k2[l]))
        yb = _diff_attention(lams, row(subln_g[l]), qb, kb, vb, lam_init=lam_init)
        x = _outproj(x, ya, yb, w_out[l].astype(BF16))
        x = _ffn(x, row(g_ffn[l]), w_up[l].astype(BF16), conv_w[l].astype(F32), row(conv_b[l]),
                 w_down[l].astype(BF16), row(g_final), final=(l == DEPTH - 1))
    return (x[:N_PROMPT].reshape(BATCH, SEQ, D_MODEL), x[N_PROMPT:].reshape(DEC_BATCH, DEC_SEQ, D_MODEL))
```

```python
import functools
import math

import jax
import jax.numpy as jnp
import numpy as np
from jax import lax
from jax.experimental import pallas as pl
from jax.experimental.pallas import tpu as pltpu

F32 = jnp.float32
BF16 = jnp.bfloat16

D_MODEL = 1024
BATCH, SEQ = 4, 4096
DEC_BATCH, DEC_SEQ = 2, 8192
DEPTH = 4
GRID_W = 64
WIN_H, WIN_W = 8, 16
NA_HEADS, NA_HEAD_DIM = 8, 64
DIFF_HEADS, DIFF_QK_DIM, DIFF_V_DIM = 4, 64, 128
SECTION = 512
IN_WIDTH = 6 * SECTION
D_FF = 2816
EPS = 1e-6

N_PROMPT = BATCH * SEQ
N_SAMPLE = DEC_BATCH * DEC_SEQ
N_TOK = N_PROMPT + N_SAMPLE
SEQ_BLOCKS = N_TOK // DEC_SEQ
PROMPT_BLOCKS = N_PROMPT // DEC_SEQ
assert DEC_SEQ == 2 * SEQ and N_PROMPT % DEC_SEQ == 0

LANES = 128
F32_SUBLANES = 8
VMEM_LIMIT_BYTES = 56 * 1024 * 1024

TM = 512
NA_QROWS = 4
NA_BROWS = 12
NA_Q = NA_QROWS * GRID_W
NA_K = NA_BROWS * GRID_W
NA_VT_ROWS = LANES + 16
NA_VT_CHUNK = 512
NA_UNROLL = 4
NEG = -1e30
DIFF_TQ = 512
DIFF_TK = 512
DIFF_VT_ROWS = DIFF_V_DIM + 16
DIFF_SAFE_SCORE = 50.0
FF_CHUNK = 256
HALO = F32_SUBLANES

_NT = (((1,), (1,)), ((), ()))


def _const_spec(shape):
    return pl.BlockSpec(shape, lambda *_: (0,) * len(shape), pipeline_mode=pl.Buffered(1))


def _params(*semantics):
    return pltpu.CompilerParams(dimension_semantics=semantics, vmem_limit_bytes=VMEM_LIMIT_BYTES)


def _rms(x, g):
    return x * lax.rsqrt(jnp.mean(x * x, axis=-1, keepdims=True) + EPS) * g


def _split_heads(x):
    first = lax.broadcasted_iota(jnp.int32, (1, LANES), 1) < LANES // 2
    zero = jnp.zeros_like(x)
    return jnp.concatenate([jnp.where(first, x, zero), jnp.where(first, zero, x)], axis=0)


def _pos_block(i):
    tiles_prompt = N_PROMPT // TM
    return jnp.where(i < tiles_prompt, i % (SEQ // TM), (i - tiles_prompt) % (DEC_SEQ // TM))


def _inproj_kernel(x_ref, g_ref, w_ref, cos_ref, sin_lo_ref, sin_hi_ref,
                   qa_ref, ka_ref, va_ref, qb_ref, kb_ref, vb_ref):
    n = _rms(x_ref[...], g_ref[...]).astype(BF16)

    def proj(j):
        return jnp.dot(n, w_ref[:, j * SECTION:(j + 1) * SECTION], preferred_element_type=F32)

    qa_ref[...] = (proj(0) * (1.0 / math.sqrt(NA_HEAD_DIM))).astype(BF16)
    ka_ref[...] = proj(1).astype(BF16)
    va_ref[...] = proj(2).astype(BF16)
    cos, sin_lo, sin_hi = cos_ref[...], sin_lo_ref[...], sin_hi_ref[...]

    def rotary(x, scale):
        heads = []
        for h in range(DIFF_HEADS):
            xh = x[:, h * LANES:(h + 1) * LANES]
            r = xh * cos + pltpu.roll(xh, 96, 1) * sin_lo + pltpu.roll(xh, 32, 1) * sin_hi
            heads.append((r * scale).astype(BF16))
        return jnp.concatenate(heads, axis=1)

    qb_ref[...] = rotary(proj(3), 1.0 / math.sqrt(DIFF_QK_DIM))
    kb_ref[...] = rotary(proj(4), 1.0)
    vb_ref[...] = proj(5).astype(BF16)


def _inproj(x, g, w, cos, sin_lo, sin_hi):
    tok = pl.BlockSpec((TM, D_MODEL), lambda i: (i, 0))
    pos = pl.BlockSpec((TM, LANES), lambda i: (_pos_block(i), 0))
    sec = pl.BlockSpec((TM, SECTION), lambda i: (i, 0))
    return pl.pallas_call(
        _inproj_kernel,
        grid=(N_TOK // TM,),
        in_specs=[tok, _const_spec((1, D_MODEL)), _const_spec((D_MODEL, IN_WIDTH)), pos, pos, pos],
        out_specs=[sec] * 6,
        out_shape=[jax.ShapeDtypeStruct((N_TOK, SECTION), BF16)] * 6,
        compiler_params=_params("parallel"),
        name="inproj",
    )(x, g, w, cos, sin_lo, sin_hi)


def _rotary_tables():
    d = DIFF_QK_DIM
    inv_freq = 1.0 / (10000.0 ** (jnp.arange(0, d, 2, dtype=F32) / d))
    ang = jnp.arange(DEC_SEQ, dtype=F32)[:, None] * inv_freq[None, :]
    cos, sin, zero = jnp.cos(ang), jnp.sin(ang), jnp.zeros_like(ang)
    tile = lambda a, b: jnp.concatenate([a, b, a, b], axis=-1)
    return tile(cos, cos), tile(-sin, zero), tile(zero, sin)


def _na_row_patterns(rows):
    nblk = rows // NA_QROWS

    def one(j):
        r0 = j * NA_QROWS
        band = int(np.clip(r0 - WIN_H // 2, 0, rows - NA_BROWS))
        r = r0 + np.arange(NA_QROWS)
        start = np.clip(r - WIN_H // 2, 0, rows - WIN_H)
        kr = band + np.arange(NA_BROWS)
        ok = (kr[None, :] >= start[:, None]) & (kr[None, :] < start[:, None] + WIN_H)
        dr = kr[None, :] - r[:, None] + (WIN_H - 1)
        return ok, np.where(ok, dr, 0)

    pats = [one(0), one(1), one(nblk - 1)]
    for j in range(1, nblk - 1):
        assert all((a == b).all() for a, b in zip(one(j), pats[1]))
    return np.stack([p[0] for p in pats]), np.stack([p[1] for p in pats])


def _na_col_table(rpb_l):
    cols = np.arange(GRID_W)
    col_start = np.clip(cols - WIN_W // 2, 0, GRID_W - WIN_W)
    ok = (cols[None, :] >= col_start[:, None]) & (cols[None, :] < col_start[:, None] + WIN_W)
    dc = cols[None, :] - cols[:, None] + (WIN_W - 1)
    onehot = ((dc[None] == np.arange(2 * WIN_W - 1)[:, None, None]) & ok[None]).astype(np.float32)
    t = jnp.einsum("hrd,dck->hrkc", rpb_l.astype(F32), jnp.asarray(onehot), precision=lax.Precision.HIGHEST)
    t = jnp.where(jnp.asarray(ok.T), t, NEG)
    return jnp.concatenate([t, t], axis=-1)


def _na_kernel(q_ref, k_ref, v_ref, tt_ref, o_ref, bias_ref, s_ref, vt_ref):
    blk = pl.program_id(1)

    @pl.when(blk == 0)
    def _():
        ok, dr = _na_row_patterns(SEQ // GRID_W)
        ok2, dr2 = _na_row_patterns(DEC_SEQ // GRID_W)
        assert (ok == ok2).all() and (dr == dr2).all()
        neg = jnp.full((GRID_W, GRID_W), NEG, F32)
        for pat in range(3):
            for hh in range(2):
                for qr in range(NA_QROWS):
                    col = hh * NA_Q + qr * GRID_W
                    half = (qr % 2) * GRID_W
                    for w in range(NA_BROWS):
                        blockval = tt_ref[hh, int(dr[pat, qr, w]), :, half:half + GRID_W] if ok[pat, qr, w] else neg
                        bias_ref[pat, w * GRID_W:(w + 1) * GRID_W, col:col + GRID_W] = blockval

    def transpose_values(c, carry):
        rows = pl.ds(pl.multiple_of(c * NA_VT_CHUNK, NA_VT_CHUNK), NA_VT_CHUNK)
        vt_ref[0:LANES, rows] = v_ref[rows, :].astype(F32).T.astype(BF16)
        return carry

    lax.fori_loop(0, DEC_SEQ // NA_VT_CHUNK, transpose_values, 0)
    vt_ref[LANES:, :] = jnp.ones((NA_VT_ROWS - LANES, DEC_SEQ), BF16)

    is_prompt = blk < PROMPT_BLOCKS
    rows = jnp.where(is_prompt, SEQ // GRID_W, DEC_SEQ // GRID_W)
    nblk_p, nblk_s = SEQ // GRID_W // NA_QROWS, DEC_SEQ // GRID_W // NA_QROWS
    first = lax.broadcasted_iota(jnp.int32, (LANES, 1), 0) < NA_HEAD_DIM

    def place(j):
        jr = jnp.where(is_prompt, j % nblk_p, j)
        base = jnp.where(is_prompt, (j // nblk_p) * (SEQ // GRID_W), 0)
        last = jnp.where(is_prompt, nblk_p, nblk_s) - 1
        r0 = jr * NA_QROWS
        band = jnp.clip(r0 - WIN_H // 2, 0, rows - NA_BROWS)
        pat = jnp.where(jr == 0, 0, jnp.where(jr == last, 2, 1))
        qoff = pl.multiple_of((base + r0) * GRID_W, NA_Q)
        koff = pl.multiple_of((base + band) * GRID_W, NA_Q)
        return qoff, koff, pat

    def scores(j, slot):
        qoff, koff, pat = place(j)
        qs = _split_heads(q_ref[pl.ds(qoff, NA_Q), :])
        s_ref[slot] = (lax.dot_general(k_ref[pl.ds(koff, NA_K), :], qs, _NT, preferred_element_type=F32)
                       + bias_ref[pat])

    def finish(j, slot):
        qoff, koff, _ = place(j)
        s = s_ref[slot]
        e = jnp.exp(s - jnp.max(s, axis=0, keepdims=True))
        o = jnp.dot(vt_ref[:, pl.ds(koff, NA_K)], e.astype(BF16), preferred_element_type=F32)
        o = o[:LANES] / o[LANES:LANES + 1]
        o_ref[pl.ds(qoff, NA_Q), :] = jnp.where(first, o[:, :NA_Q], o[:, NA_Q:]).T.astype(BF16)

    def body(g, carry):
        for u in range(NA_UNROLL):
            j = g * NA_UNROLL + u
            scores(jnp.minimum(j + 1, nblk_s - 1), (u + 1) % 2)
            finish(j, u % 2)
        return carry

    scores(0, 0)
    lax.fori_loop(0, nblk_s // NA_UNROLL, body, 0)


def _na_attention(qa, ka, va, tt):
    blk = pl.BlockSpec((DEC_SEQ, LANES), lambda hp, b: (b, hp))
    return pl.pallas_call(
        _na_kernel,
        grid=(NA_HEADS // 2, SEQ_BLOCKS),
        in_specs=[blk, blk, blk,
                  pl.BlockSpec((2, 2 * WIN_H - 1, GRID_W, LANES), lambda hp, b: (hp, 0, 0, 0))],
        out_specs=blk,
        out_shape=jax.ShapeDtypeStruct((N_TOK, SECTION), BF16),
        scratch_shapes=[pltpu.VMEM((3, NA_K, 2 * NA_Q), F32), pltpu.VMEM((2, NA_K, 2 * NA_Q), F32),
                        pltpu.VMEM((NA_VT_ROWS, DEC_SEQ), BF16)],
        compiler_params=_params("arbitrary", "arbitrary"),
        name="na_attention",
    )(qa, ka, va, tt)


def _half_sums(x):
    first = lax.broadcasted_iota(jnp.int32, (1, LANES), 1) < LANES // 2
    zero = jnp.zeros_like(x)
    return (jnp.sum(jnp.where(first, x, zero), axis=-1, keepdims=True),
            jnp.sum(jnp.where(first, zero, x), axis=-1, keepdims=True))


def _diff_kernel(lq1_ref, lk1_ref, lq2_ref, lk2_ref, g_ref, q_ref, k_ref, v_ref, o_ref, s_ref, acc_ref, m_ref,
                 vt_ref, *, lam_init):
    is_prompt = pl.program_id(0) < PROMPT_BLOCKS
    n_qblocks = DEC_SEQ // DIFF_TQ

    def transpose_values(c, carry):
        rows = pl.ds(pl.multiple_of(c * DIFF_TQ, DIFF_TQ), DIFF_TQ)
        vt_ref[0:DIFF_V_DIM, rows] = v_ref[rows, :].astype(F32).T.astype(BF16)
        return carry

    lax.fori_loop(0, n_qblocks, transpose_values, 0)
    vt_ref[DIFF_V_DIM:, :] = jnp.ones((DIFF_VT_ROWS - DIFF_V_DIM, DEC_SEQ), BF16)

    def key_rows(qi, c, seq):
        base = (qi // (seq // DIFF_TQ)) * seq
        return pl.ds(pl.multiple_of(base + c * DIFF_TK, DIFF_TK), DIFF_TK)

    def scores(qi, c, slot, seq):
        q = q_ref[pl.ds(pl.multiple_of(qi * DIFF_TQ, DIFF_TQ), DIFF_TQ), :]
        s_ref[slot] = lax.dot_general(k_ref[key_rows(qi, c, seq), :], _split_heads(q), _NT,
                                      preferred_element_type=F32)

    def accumulate(qi, c, slot, seq, shifted):
        s = s_ref[slot]
        p = jnp.exp(s - m_ref[0:1, :] if shifted else s).astype(BF16)
        acc_ref[...] += jnp.dot(vt_ref[:, key_rows(qi, c, seq)], p, preferred_element_type=F32)

    def finish(qi):
        acc = acc_ref[...]
        sm1 = acc[:DIFF_V_DIM, :DIFF_TQ] / acc[DIFF_V_DIM:DIFF_V_DIM + 1, :DIFF_TQ]
        sm2 = acc[:DIFF_V_DIM, DIFF_TQ:] / acc[DIFF_V_DIM:DIFF_V_DIM + 1, DIFF_TQ:]
        lam = (jnp.exp(jnp.sum(lq1_ref[...] * lk1_ref[...], axis=-1, keepdims=True))
               - jnp.exp(jnp.sum(lq2_ref[...] * lk2_ref[...], axis=-1, keepdims=True)) + lam_init)
        d = sm1 - lam * sm2
        d = d * lax.rsqrt(jnp.mean(d * d, axis=0, keepdims=True) + EPS)
        y = d.T * g_ref[...] * (1.0 - lam_init)
        o_ref[pl.ds(pl.multiple_of(qi * DIFF_TQ, DIFF_TQ), DIFF_TQ), :] = y.astype(BF16)
        acc_ref[...] = jnp.zeros_like(acc_ref)

    def max_sq_norms(ref):
        def body(c, carry):
            x = ref[pl.ds(pl.multiple_of(c * DIFF_TQ, DIFF_TQ), DIFF_TQ), :].astype(F32)
            n1, n2 = _half_sums(x * x)
            return (jnp.maximum(carry[0], jnp.max(n1, axis=0, keepdims=True)),
                    jnp.maximum(carry[1], jnp.max(n2, axis=0, keepdims=True)))
        zero = jnp.zeros((1, 1), F32)
        return lax.fori_loop(0, n_qblocks, body, (zero, zero))

    (q1, q2), (k1, k2) = max_sq_norms(q_ref), max_sq_norms(k_ref)
    needs_max = jnp.max(jnp.maximum(q1 * k1, q2 * k2)) > DIFF_SAFE_SCORE ** 2
    acc_ref[...] = jnp.zeros_like(acc_ref)

    def unshifted(seq):
        nchunks = seq // DIFF_TK
        assert nchunks % 2 == 0

        def body(qi, carry):
            for c in range(nchunks):
                if c + 1 < nchunks:
                    scores(qi, c + 1, (c + 1) % 2, seq)
                else:
                    scores(jnp.minimum(qi + 1, n_qblocks - 1), 0, 0, seq)
                accumulate(qi, c, c % 2, seq, False)
            finish(qi)
            return carry

        def run():
            scores(0, 0, 0, seq)
            lax.fori_loop(0, n_qblocks, body, 0)
        return run

    def shifted(seq):
        nchunks = seq // DIFF_TK

        def qblock(qi, carry):
            def col_max(c, m):
                scores(qi, c, 0, seq)
                return jnp.maximum(m, jnp.max(s_ref[0], axis=0, keepdims=True))

            m = lax.fori_loop(0, nchunks, col_max, jnp.full((1, 2 * DIFF_TQ), NEG, F32))
            m_ref[...] = jnp.broadcast_to(m, m_ref.shape)

            def chunk(c, carry):
                scores(qi, c, 0, seq)
                accumulate(qi, c, 0, seq, True)
                return carry

            lax.fori_loop(0, nchunks, chunk, 0)
            finish(qi)
            return carry

        def run():
            lax.fori_loop(0, n_qblocks, qblock, 0)
        return run

    lax.cond(needs_max,
             lambda: lax.cond(is_prompt, shifted(SEQ), shifted(DEC_SEQ)),
             lambda: lax.cond(is_prompt, unshifted(SEQ), unshifted(DEC_SEQ)))


def _diff_attention(lams, g, qb, kb, vb, *, lam_init):
    vec = _const_spec((1, DIFF_QK_DIM))
    blk = pl.BlockSpec((DEC_SEQ, LANES), lambda b, h: (b, h))
    return pl.pallas_call(
        functools.partial(_diff_kernel, lam_init=lam_init),
        grid=(SEQ_BLOCKS, DIFF_HEADS),
        in_specs=[vec, vec, vec, vec, _const_spec((1, DIFF_V_DIM)), blk, blk, blk],
        out_specs=blk,
        out_shape=jax.ShapeDtypeStruct((N_TOK, SECTION), BF16),
        scratch_shapes=[pltpu.VMEM((2, DIFF_TK, 2 * DIFF_TQ), F32), pltpu.VMEM((DIFF_VT_ROWS, 2 * DIFF_TQ), F32),
                        pltpu.VMEM((F32_SUBLANES, 2 * DIFF_TQ), F32), pltpu.VMEM((DIFF_VT_ROWS, DEC_SEQ), BF16)],
        compiler_params=_params("parallel", "parallel"),
        name="diff_attention",
    )(*lams, g, qb, kb, vb)


def _outproj_kernel(x_ref, ya_ref, yb_ref, w_ref, o_ref):
    o_ref[...] = (x_ref[...]
                  + jnp.dot(ya_ref[...], w_ref[:SECTION, :], preferred_element_type=F32)
                  + jnp.dot(yb_ref[...], w_ref[SECTION:, :], preferred_element_type=F32))


def _outproj(x, ya, yb, w):
    tok = pl.BlockSpec((TM, D_MODEL), lambda i: (i, 0))
    sec = pl.BlockSpec((TM, SECTION), lambda i: (i, 0))
    return pl.pallas_call(
        _outproj_kernel,
        grid=(N_TOK // TM,),
        in_specs=[tok, sec, sec, _const_spec((2 * SECTION, D_MODEL))],
        out_specs=tok,
        out_shape=jax.ShapeDtypeStruct((N_TOK, D_MODEL), F32),
        compiler_params=_params("parallel"),
        name="outproj",
    )(x, ya, yb, w)


def _ffn_kernel(x_ref, xp_ref, xn_ref, g_ref, wup_ref, cw_ref, cb_ref, wdn_ref, gf_ref, o_ref, h_ref, *, final):
    i = pl.program_id(0)
    tiles_prompt = N_PROMPT // TM
    per_seq = jnp.where(i < tiles_prompt, SEQ // TM, DEC_SEQ // TM)
    has_prev = (i % per_seq) != 0
    has_next = (i % per_seq) != per_seq - 1

    x = x_ref[...]
    xe = jnp.concatenate([xp_ref[...], x, xn_ref[...]], axis=0)
    n = _rms(xe, g_ref[...]).astype(BF16)
    ext = TM + 2 * HALO
    row = lax.broadcasted_iota(jnp.int32, (ext, 1), 0)
    keep = ((row >= HALO) | has_prev) & ((row < HALO + TM) | has_next)

    for c in range(D_FF // FF_CHUNK):
        lo = c * FF_CHUNK
        gate = jnp.dot(n, wup_ref[:, lo:lo + FF_CHUNK], preferred_element_type=F32)
        val = jnp.dot(n, wup_ref[:, D_FF + lo:D_FF + lo + FF_CHUNK], preferred_element_type=F32)
        gate = jnp.where(keep, gate, 0.0)
        cw = cw_ref[:, lo:lo + FF_CHUNK]
        conv = (pltpu.roll(gate, 1, 0) * cw[0:1] + gate * cw[1:2] + pltpu.roll(gate, ext - 1, 0) * cw[2:3]
                + cb_ref[:, lo:lo + FF_CHUNK])
        act = 0.5 * conv * (1.0 + lax.erf(conv * (1.0 / math.sqrt(2.0))))
        h_ref[:, lo:lo + FF_CHUNK] = (act * val)[HALO:HALO + TM].astype(BF16)

    y = x + jnp.dot(h_ref[...], wdn_ref[...], preferred_element_type=F32)
    o_ref[...] = _rms(y, gf_ref[...]) if final else y


def _ffn(x, g, wup, cw, cb, wdn, gf, *, final):
    tok = pl.BlockSpec((TM, D_MODEL), lambda i: (i, 0))
    blocks_per_tile = TM // HALO
    prev = pl.BlockSpec((HALO, D_MODEL), lambda i: (jnp.maximum(i * blocks_per_tile - 1, 0), 0))
    nxt = pl.BlockSpec((HALO, D_MODEL), lambda i: (jnp.minimum((i + 1) * blocks_per_tile, N_TOK // HALO - 1), 0))
    return pl.pallas_call(
        functools.partial(_ffn_kernel, final=final),
        grid=(N_TOK // TM,),
        in_specs=[tok, prev, nxt, _const_spec((1, D_MODEL)), _const_spec((D_MODEL, 2 * D_FF)),
                  _const_spec((3, D_FF)), _const_spec((1, D_FF)), _const_spec((D_FF, D_MODEL)),
                  _const_spec((1, D_MODEL))],
        out_specs=tok,
        out_shape=jax.ShapeDtypeStruct((N_TOK, D_MODEL), F32),
        scratch_shapes=[pltpu.VMEM((TM, D_FF), BF16)],
        compiler_params=_params("parallel"),
        name="ffn_final" if final else "ffn",
    )(x, x, x, g, wup, cw, cb, wdn, gf)


def kernel(x_prompt, x_sample, g_attn, w_in, rpb, lam_q1, lam_k1, lam_q2, lam_k2, subln_g, w_out, g_ffn, w_up,
           conv_w, conv_b, w_down, g_final):
    x = jnp.concatenate([x_prompt.reshape(N_PROMPT, D_MODEL), x_sample.reshape(N_SAMPLE, D_MODEL)], axis=0)
    cos, sin_lo, sin_hi = _rotary_tables()
    row = lambda v: v.reshape(1, -1).astype(F32)
    for l in range(DEPTH):
        lam_init = 0.8 - 0.6 * math.exp(-0.3 * l)
        qa, ka, va, qb, kb, vb = _inproj(x, row(g_attn[l]), w_in[l].astype(BF16), cos, sin_lo, sin_hi)
        ya = _na_attention(qa, ka, va, _na_col_table(rpb[l]))
        lams = (row(lam_q1[l]), row(lam_k1[l]), row(lam_q2[l]), row(lam_k2[l]))
        yb = _diff_attention(lams, row(subln_g[l]), qb, kb, vb, lam_init=lam_init)
        x = _outproj(x, ya, yb, w_out[l].astype(BF16))
        x = _ffn(x, row(g_ffn[l]), w_up[l].astype(BF16), conv_w[l].astype(F32), row(conv_b[l]),
                 w_down[l].astype(BF16), row(g_final), final=(l == DEPTH - 1))
    return (x[:N_PROMPT].reshape(BATCH, SEQ, D_MODEL), x[N_PROMPT:].reshape(DEC_BATCH, DEC_SEQ, D_MODEL))
```

```python
import functools
import math

import jax
import jax.numpy as jnp
import numpy as np
from jax import lax
from jax.experimental import pallas as pl
from jax.experimental.pallas import tpu as pltpu

F32 = jnp.float32
BF16 = jnp.bfloat16

D_MODEL = 1024
BATCH, SEQ = 4, 4096
DEC_BATCH, DEC_SEQ = 2, 8192
DEPTH = 4
GRID_W = 64
WIN_H, WIN_W = 8, 16
NA_HEADS, NA_HEAD_DIM = 8, 64
DIFF_HEADS, DIFF_QK_DIM, DIFF_V_DIM = 4, 64, 128
SECTION = 512
IN_WIDTH = 6 * SECTION
D_FF = 2816
EPS = 1e-6

N_PROMPT = BATCH * SEQ
N_SAMPLE = DEC_BATCH * DEC_SEQ
N_TOK = N_PROMPT + N_SAMPLE
SEQ_BLOCKS = N_TOK // DEC_SEQ
PROMPT_BLOCKS = N_PROMPT // DEC_SEQ
assert DEC_SEQ == 2 * SEQ and N_PROMPT % DEC_SEQ == 0

LANES = 128
F32_SUBLANES = 8
VMEM_LIMIT_BYTES = 56 * 1024 * 1024

TM = 512
NA_QROWS = 4
NA_BROWS = 12
NA_Q = NA_QROWS * GRID_W
NA_K = NA_BROWS * GRID_W
NA_UNROLL = 4
NEG = -1e30
DIFF_TQ = 512
DIFF_TK = 1024
DIFF_VT_ROWS = DIFF_V_DIM + 16
DIFF_SAFE_SCORE = 50.0
FF_CHUNK = 256
HALO = F32_SUBLANES
Y_HALO = 16

_NT = (((1,), (1,)), ((), ()))


def _const_spec(shape):
    return pl.BlockSpec(shape, lambda *_: (0,) * len(shape), pipeline_mode=pl.Buffered(1))


def _params(*semantics):
    return pltpu.CompilerParams(dimension_semantics=semantics, vmem_limit_bytes=VMEM_LIMIT_BYTES)


def _rms(x, g):
    return x * lax.rsqrt(jnp.mean(x * x, axis=-1, keepdims=True) + EPS) * g


def _split_heads(x):
    first = lax.broadcasted_iota(jnp.int32, (1, LANES), 1) < LANES // 2
    zero = jnp.zeros_like(x)
    return jnp.concatenate([jnp.where(first, x, zero), jnp.where(first, zero, x)], axis=0)


def _pos_block(i):
    tiles_prompt = N_PROMPT // TM
    return jnp.where(i < tiles_prompt, i % (SEQ // TM), (i - tiles_prompt) % (DEC_SEQ // TM))


def _inproj_kernel(*refs, first_layer):
    if first_layer:
        xp_ref, xs_ref, g_ref, w_ref, cos_ref, sin_lo_ref, sin_hi_ref = refs[:7]
        qa_ref, ka_ref, va_ref, qb_ref, kb_ref, vb_ref, xm_ref = refs[7:]
        x = jnp.where(pl.program_id(0) < N_PROMPT // TM, xp_ref[...], xs_ref[...])
        xm_ref[...] = x
    else:
        x_ref, g_ref, w_ref, cos_ref, sin_lo_ref, sin_hi_ref = refs[:6]
        qa_ref, ka_ref, va_ref, qb_ref, kb_ref, vb_ref = refs[6:]
        x = x_ref[...]
    n = _rms(x, g_ref[...]).astype(BF16)

    def proj(j):
        return jnp.dot(n, w_ref[:, j * SECTION:(j + 1) * SECTION], preferred_element_type=F32)

    qa_ref[...] = (proj(0) * (1.0 / math.sqrt(NA_HEAD_DIM))).astype(BF16)
    ka_ref[...] = proj(1).astype(BF16)
    va_ref[...] = proj(2).astype(BF16)
    cos, sin_lo, sin_hi = cos_ref[...], sin_lo_ref[...], sin_hi_ref[...]

    def rotary(x, scale):
        heads = []
        for h in range(DIFF_HEADS):
            xh = x[:, h * LANES:(h + 1) * LANES]
            r = xh * cos + pltpu.roll(xh, 96, 1) * sin_lo + pltpu.roll(xh, 32, 1) * sin_hi
            heads.append((r * scale).astype(BF16))
        return jnp.concatenate(heads, axis=1)

    qb_ref[...] = rotary(proj(3), 1.0 / math.sqrt(DIFF_QK_DIM))
    kb_ref[...] = rotary(proj(4), 1.0)
    vb_ref[...] = proj(5).astype(BF16)


def _inproj(xs, g, w, cos, sin_lo, sin_hi):
    first_layer = len(xs) == 2
    tiles_prompt = N_PROMPT // TM
    tok = pl.BlockSpec((TM, D_MODEL), lambda i: (i, 0))
    pos = pl.BlockSpec((TM, LANES), lambda i: (_pos_block(i), 0))
    sec = pl.BlockSpec((TM, SECTION), lambda i: (i, 0))
    if first_layer:
        x_specs = [pl.BlockSpec((TM, D_MODEL), lambda i: (jnp.minimum(i, tiles_prompt - 1), 0)),
                   pl.BlockSpec((TM, D_MODEL), lambda i: (jnp.maximum(i - tiles_prompt, 0), 0))]
    else:
        x_specs = [tok]
    return pl.pallas_call(
        functools.partial(_inproj_kernel, first_layer=first_layer),
        grid=(N_TOK // TM,),
        in_specs=x_specs + [_const_spec((1, D_MODEL)), _const_spec((D_MODEL, IN_WIDTH)), pos, pos, pos],
        out_specs=[sec] * 6 + [tok] * first_layer,
        out_shape=([jax.ShapeDtypeStruct((N_TOK, SECTION), BF16)] * 6
                   + [jax.ShapeDtypeStruct((N_TOK, D_MODEL), F32)] * first_layer),
        compiler_params=_params("parallel"),
        name="inproj_first" if first_layer else "inproj",
    )(*xs, g, w, cos, sin_lo, sin_hi)


def _rotary_tables():
    d = DIFF_QK_DIM
    inv_freq = 1.0 / (10000.0 ** (jnp.arange(0, d, 2, dtype=F32) / d))
    ang = jnp.arange(DEC_SEQ, dtype=F32)[:, None] * inv_freq[None, :]
    cos, sin, zero = jnp.cos(ang), jnp.sin(ang), jnp.zeros_like(ang)
    tile = lambda a, b: jnp.concatenate([a, b, a, b], axis=-1)
    return tile(cos, cos), tile(-sin, zero), tile(zero, sin)


def _na_row_patterns(rows):
    nblk = rows // NA_QROWS

    def one(j):
        r0 = j * NA_QROWS
        band = int(np.clip(r0 - WIN_H // 2, 0, rows - NA_BROWS))
        r = r0 + np.arange(NA_QROWS)
        start = np.clip(r - WIN_H // 2, 0, rows - WIN_H)
        kr = band + np.arange(NA_BROWS)
        ok = (kr[None, :] >= start[:, None]) & (kr[None, :] < start[:, None] + WIN_H)
        dr = kr[None, :] - r[:, None] + (WIN_H - 1)
        return ok, np.where(ok, dr, 0)

    pats = [one(0), one(1), one(nblk - 1)]
    for j in range(1, nblk - 1):
        assert all((a == b).all() for a, b in zip(one(j), pats[1]))
    return np.stack([p[0] for p in pats]), np.stack([p[1] for p in pats])


def _na_col_table(rpb_l):
    cols = np.arange(GRID_W)
    col_start = np.clip(cols - WIN_W // 2, 0, GRID_W - WIN_W)
    ok = (cols[None, :] >= col_start[:, None]) & (cols[None, :] < col_start[:, None] + WIN_W)
    dc = cols[None, :] - cols[:, None] + (WIN_W - 1)
    onehot = ((dc[None] == np.arange(2 * WIN_W - 1)[:, None, None]) & ok[None]).astype(np.float32)
    t = jnp.einsum("hrd,dck->hrck", rpb_l.astype(F32), jnp.asarray(onehot), precision=lax.Precision.HIGHEST)
    t = jnp.where(jnp.asarray(ok), t, NEG)
    return jnp.concatenate([t, t], axis=-1)


def _na_kernel(q_ref, k_ref, v_ref, tt_ref, o_ref, bias_ref, s_ref):
    blk = pl.program_id(1)

    @pl.when(blk == 0)
    def _():
        ok, dr = _na_row_patterns(SEQ // GRID_W)
        ok2, dr2 = _na_row_patterns(DEC_SEQ // GRID_W)
        assert (ok == ok2).all() and (dr == dr2).all()
        neg = jnp.full((GRID_W, GRID_W), NEG, F32)
        for pat in range(3):
            for hh in range(2):
                for qr in range(NA_QROWS):
                    r = hh * NA_Q + qr * GRID_W
                    for w in range(NA_BROWS):
                        half = (w % 2) * GRID_W
                        blockval = tt_ref[hh, int(dr[pat, qr, w]), :, half:half + GRID_W] if ok[pat, qr, w] else neg
                        bias_ref[pat, r:r + GRID_W, w * GRID_W:(w + 1) * GRID_W] = blockval

    is_prompt = blk < PROMPT_BLOCKS
    rows = jnp.where(is_prompt, SEQ // GRID_W, DEC_SEQ // GRID_W)
    nblk_p, nblk_s = SEQ // GRID_W // NA_QROWS, DEC_SEQ // GRID_W // NA_QROWS
    first = lax.broadcasted_iota(jnp.int32, (1, LANES), 1) < NA_HEAD_DIM

    def place(j):
        jr = jnp.where(is_prompt, j % nblk_p, j)
        base = jnp.where(is_prompt, (j // nblk_p) * (SEQ // GRID_W), 0)
        last = jnp.where(is_prompt, nblk_p, nblk_s) - 1
        r0 = jr * NA_QROWS
        band = jnp.clip(r0 - WIN_H // 2, 0, rows - NA_BROWS)
        pat = jnp.where(jr == 0, 0, jnp.where(jr == last, 2, 1))
        qoff = pl.multiple_of((base + r0) * GRID_W, NA_Q)
        koff = pl.multiple_of((base + band) * GRID_W, GRID_W)
        return qoff, koff, pat

    def scores(j, slot):
        qoff, koff, pat = place(j)
        qs = _split_heads(q_ref[pl.ds(qoff, NA_Q), :])
        s_ref[slot] = (lax.dot_general(qs, k_ref[pl.ds(koff, NA_K), :], _NT, preferred_element_type=F32)
                       + bias_ref[pat])

    def finish(j, slot):
        qoff, koff, _ = place(j)
        s = s_ref[slot]
        e = jnp.exp(s - jnp.max(s, axis=-1, keepdims=True))
        l = jnp.sum(e, axis=-1, keepdims=True)
        o = jnp.dot(e.astype(BF16), v_ref[pl.ds(koff, NA_K), :], preferred_element_type=F32) / l
        o_ref[pl.ds(qoff, NA_Q), :] = jnp.where(first, o[:NA_Q], o[NA_Q:]).astype(BF16)

    def body(g, carry):
        for u in range(NA_UNROLL):
            j = g * NA_UNROLL + u
            scores(jnp.minimum(j + 1, nblk_s - 1), (u + 1) % 2)
            finish(j, u % 2)
        return carry

    scores(0, 0)
    lax.fori_loop(0, nblk_s // NA_UNROLL, body, 0)


def _na_attention(qa, ka, va, tt):
    blk = pl.BlockSpec((DEC_SEQ, LANES), lambda hp, b: (b, hp))
    return pl.pallas_call(
        _na_kernel,
        grid=(NA_HEADS // 2, SEQ_BLOCKS),
        in_specs=[blk, blk, blk,
                  pl.BlockSpec((2, 2 * WIN_H - 1, GRID_W, LANES), lambda hp, b: (hp, 0, 0, 0))],
        out_specs=blk,
        out_shape=jax.ShapeDtypeStruct((N_TOK, SECTION), BF16),
        scratch_shapes=[pltpu.VMEM((3, 2 * NA_Q, NA_K), F32), pltpu.VMEM((2, 2 * NA_Q, NA_K), F32)],
        compiler_params=_params("arbitrary", "arbitrary"),
        name="na_attention",
    )(qa, ka, va, tt)


def _half_sums(x):
    first = lax.broadcasted_iota(jnp.int32, (1, LANES), 1) < LANES // 2
    zero = jnp.zeros_like(x)
    return (jnp.sum(jnp.where(first, x, zero), axis=-1, keepdims=True),
            jnp.sum(jnp.where(first, zero, x), axis=-1, keepdims=True))


def _diff_kernel(lq1_ref, lk1_ref, lq2_ref, lk2_ref, g_ref, q_ref, k_ref, v_ref, o_ref, s_ref, acc_ref, m_ref,
                 vt_ref, *, lam_init):
    is_prompt = pl.program_id(0) < PROMPT_BLOCKS
    n_qblocks = DEC_SEQ // DIFF_TQ

    def transpose_values(c, carry):
        rows = pl.ds(pl.multiple_of(c * DIFF_TQ, DIFF_TQ), DIFF_TQ)
        vt_ref[0:DIFF_V_DIM, rows] = v_ref[rows, :].astype(F32).T.astype(BF16)
        return carry

    lax.fori_loop(0, n_qblocks, transpose_values, 0)
    vt_ref[DIFF_V_DIM:, :] = jnp.ones((DIFF_VT_ROWS - DIFF_V_DIM, DEC_SEQ), BF16)

    def key_rows(qi, c, seq):
        base = (qi // (seq // DIFF_TQ)) * seq
        return pl.ds(pl.multiple_of(base + c * DIFF_TK, DIFF_TK), DIFF_TK)

    def scores(qi, c, slot, seq):
        q = q_ref[pl.ds(pl.multiple_of(qi * DIFF_TQ, DIFF_TQ), DIFF_TQ), :]
        s_ref[slot] = lax.dot_general(k_ref[key_rows(qi, c, seq), :], _split_heads(q), _NT,
                                      preferred_element_type=F32)

    def accumulate(qi, c, slot, seq, shifted):
        s = s_ref[slot]
        p = jnp.exp(s - m_ref[0:1, :] if shifted else s).astype(BF16)
        acc_ref[...] += jnp.dot(vt_ref[:, key_rows(qi, c, seq)], p, preferred_element_type=F32)

    def finish(qi):
        acc = acc_ref[...]
        sm1 = acc[:DIFF_V_DIM, :DIFF_TQ] / acc[DIFF_V_DIM:DIFF_V_DIM + 1, :DIFF_TQ]
        sm2 = acc[:DIFF_V_DIM, DIFF_TQ:] / acc[DIFF_V_DIM:DIFF_V_DIM + 1, DIFF_TQ:]
        lam = (jnp.exp(jnp.sum(lq1_ref[...] * lk1_ref[...], axis=-1, keepdims=True))
               - jnp.exp(jnp.sum(lq2_ref[...] * lk2_ref[...], axis=-1, keepdims=True)) + lam_init)
        d = sm1 - lam * sm2
        d = d * lax.rsqrt(jnp.mean(d * d, axis=0, keepdims=True) + EPS)
        y = d.T * g_ref[...] * (1.0 - lam_init)
        o_ref[pl.ds(pl.multiple_of(qi * DIFF_TQ, DIFF_TQ), DIFF_TQ), :] = y.astype(BF16)
        acc_ref[...] = jnp.zeros_like(acc_ref)

    def max_sq_norms(ref):
        def body(c, carry):
            x = ref[pl.ds(pl.multiple_of(c * DIFF_TQ, DIFF_TQ), DIFF_TQ), :].astype(F32)
            n1, n2 = _half_sums(x * x)
            return (jnp.maximum(carry[0], jnp.max(n1, axis=0, keepdims=True)),
                    jnp.maximum(carry[1], jnp.max(n2, axis=0, keepdims=True)))
        zero = jnp.zeros((1, 1), F32)
        return lax.fori_loop(0, n_qblocks, body, (zero, zero))

    (q1, q2), (k1, k2) = max_sq_norms(q_ref), max_sq_norms(k_ref)
    needs_max = jnp.max(jnp.maximum(q1 * k1, q2 * k2)) > DIFF_SAFE_SCORE ** 2
    acc_ref[...] = jnp.zeros_like(acc_ref)

    def unshifted(seq):
        nchunks = seq // DIFF_TK
        assert nchunks % 2 == 0

        def body(qi, carry):
            for c in range(nchunks):
                if c + 1 < nchunks:
                    scores(qi, c + 1, (c + 1) % 2, seq)
                else:
                    scores(jnp.minimum(qi + 1, n_qblocks - 1), 0, 0, seq)
                accumulate(qi, c, c % 2, seq, False)
            finish(qi)
            return carry

        def run():
            scores(0, 0, 0, seq)
            lax.fori_loop(0, n_qblocks, body, 0)
        return run

    def shifted(seq):
        nchunks = seq // DIFF_TK

        def qblock(qi, carry):
            def col_max(c, m):
                scores(qi, c, 0, seq)
                return jnp.maximum(m, jnp.max(s_ref[0], axis=0, keepdims=True))

            m = lax.fori_loop(0, nchunks, col_max, jnp.full((1, 2 * DIFF_TQ), NEG, F32))
            m_ref[...] = jnp.broadcast_to(m, m_ref.shape)

            def chunk(c, carry):
                scores(qi, c, 0, seq)
                accumulate(qi, c, 0, seq, True)
                return carry

            lax.fori_loop(0, nchunks, chunk, 0)
            finish(qi)
            return carry

        def run():
            lax.fori_loop(0, n_qblocks, qblock, 0)
        return run

    lax.cond(needs_max,
             lambda: lax.cond(is_prompt, shifted(SEQ), shifted(DEC_SEQ)),
             lambda: lax.cond(is_prompt, unshifted(SEQ), unshifted(DEC_SEQ)))


def _diff_attention(lams, g, qb, kb, vb, *, lam_init):
    vec = _const_spec((1, DIFF_QK_DIM))
    blk = pl.BlockSpec((DEC_SEQ, LANES), lambda b, h: (b, h))
    return pl.pallas_call(
        functools.partial(_diff_kernel, lam_init=lam_init),
        grid=(SEQ_BLOCKS, DIFF_HEADS),
        in_specs=[vec, vec, vec, vec, _const_spec((1, DIFF_V_DIM)), blk, blk, blk],
        out_specs=blk,
        out_shape=jax.ShapeDtypeStruct((N_TOK, SECTION), BF16),
        scratch_shapes=[pltpu.VMEM((2, DIFF_TK, 2 * DIFF_TQ), F32), pltpu.VMEM((DIFF_VT_ROWS, 2 * DIFF_TQ), F32),
                        pltpu.VMEM((F32_SUBLANES, 2 * DIFF_TQ), F32), pltpu.VMEM((DIFF_VT_ROWS, DEC_SEQ), BF16)],
        compiler_params=_params("parallel", "parallel"),
        name="diff_attention",
    )(*lams, g, qb, kb, vb)


def _ffn_kernel(x_ref, xp_ref, xn_ref, ya_ref, yap_ref, yan_ref, yb_ref, ybp_ref, ybn_ref, wo_ref, g_ref, wup_ref,
                cw_ref, cb_ref, wdn_ref, gf_ref, o_ref, h_ref, *, final):
    i = pl.program_id(0)
    tiles_prompt = N_PROMPT // TM
    per_seq = jnp.where(i < tiles_prompt, SEQ // TM, DEC_SEQ // TM)
    has_prev = (i % per_seq) != 0
    has_next = (i % per_seq) != per_seq - 1

    ext = TM + 2 * HALO
    ya = jnp.concatenate([yap_ref[...], ya_ref[...], yan_ref[...]], axis=0)
    yb = jnp.concatenate([ybp_ref[...], yb_ref[...], ybn_ref[...]], axis=0)
    attn = (jnp.dot(ya, wo_ref[:SECTION, :], preferred_element_type=F32)
            + jnp.dot(yb, wo_ref[SECTION:, :], preferred_element_type=F32))
    xe = (jnp.concatenate([xp_ref[...], x_ref[...], xn_ref[...]], axis=0)
          + attn[Y_HALO - HALO:Y_HALO - HALO + ext])
    x = xe[HALO:HALO + TM]
    n = _rms(xe, g_ref[...]).astype(BF16)
    row = lax.broadcasted_iota(jnp.int32, (ext, 1), 0)
    keep = ((row >= HALO) | has_prev) & ((row < HALO + TM) | has_next)

    for c in range(D_FF // FF_CHUNK):
        lo = c * FF_CHUNK
        gate = jnp.dot(n, wup_ref[:, lo:lo + FF_CHUNK], preferred_element_type=F32)
        val = jnp.dot(n, wup_ref[:, D_FF + lo:D_FF + lo + FF_CHUNK], preferred_element_type=F32)
        gate = jnp.where(keep, gate, 0.0)
        cw = cw_ref[:, lo:lo + FF_CHUNK]
        conv = (pltpu.roll(gate, 1, 0) * cw[0:1] + gate * cw[1:2] + pltpu.roll(gate, ext - 1, 0) * cw[2:3]
                + cb_ref[:, lo:lo + FF_CHUNK])
        act = 0.5 * conv * (1.0 + lax.erf(conv * (1.0 / math.sqrt(2.0))))
        h_ref[:, lo:lo + FF_CHUNK] = (act * val)[HALO:HALO + TM].astype(BF16)

    y = x + jnp.dot(h_ref[...], wdn_ref[...], preferred_element_type=F32)
    o_ref[...] = _rms(y, gf_ref[...]) if final else y


def _ffn(x, ya, yb, wo, g, wup, cw, cb, wdn, gf, *, final):
    def tile_and_halos(width, halo):
        per_tile = TM // halo
        return [pl.BlockSpec((TM, width), lambda i: (i, 0)),
                pl.BlockSpec((halo, width), lambda i: (jnp.maximum(i * per_tile - 1, 0), 0)),
                pl.BlockSpec((halo, width), lambda i: (jnp.minimum((i + 1) * per_tile, N_TOK // halo - 1), 0))]

    x_specs = tile_and_halos(D_MODEL, HALO)
    y_specs = tile_and_halos(SECTION, Y_HALO)
    return pl.pallas_call(
        functools.partial(_ffn_kernel, final=final),
        grid=(N_TOK // TM,),
        in_specs=x_specs + y_specs + y_specs + [
            _const_spec((2 * SECTION, D_MODEL)), _const_spec((1, D_MODEL)), _const_spec((D_MODEL, 2 * D_FF)),
            _const_spec((3, D_FF)), _const_spec((1, D_FF)), _const_spec((D_FF, D_MODEL)), _const_spec((1, D_MODEL))],
        out_specs=x_specs[0],
        out_shape=jax.ShapeDtypeStruct((N_TOK, D_MODEL), F32),
        scratch_shapes=[pltpu.VMEM((TM, D_FF), BF16)],
        compiler_params=_params("parallel"),
        name="ffn_final" if final else "ffn",
    )(x, x, x, ya, ya, ya, yb, yb, yb, wo, g, wup, cw, cb, wdn, gf)


def kernel(x_prompt, x_sample, g_attn, w_in, rpb, lam_q1, lam_k1, lam_q2, lam_k2, subln_g, w_out, g_ffn, w_up,
           conv_w, conv_b, w_down, g_final):
    cos, sin_lo, sin_hi = _rotary_tables()
    row = lambda v: v.reshape(1, -1).astype(F32)
    xs = (x_prompt.reshape(N_PROMPT, D_MODEL), x_sample.reshape(N_SAMPLE, D_MODEL))
    for l in range(DEPTH):
        lam_init = 0.8 - 0.6 * math.exp(-0.3 * l)
        qa, ka, va, qb, kb, vb, *merged = _inproj(xs, row(g_attn[l]), w_in[l].astype(BF16), cos, sin_lo, sin_hi)
        x = merged[0] if merged else xs[0]
        ya = _na_attention(qa, ka, va, _na_col_table(rpb[l]))
        lams = (row(lam_q1[l]), row(lam_k1[l]), row(lam_q2[l]), row(lam_k2[l]))
        yb = _diff_attention(lams, row(subln_g[l]), qb, kb, vb, lam_init=lam_init)
        x = _ffn(x, ya, yb, w_out[l].astype(BF16), row(g_ffn[l]), w_up[l].astype(BF16), conv_w[l].astype(F32),
                 row(conv_b[l]), w_down[l].astype(BF16), row(g_final), final=(l == DEPTH - 1))
        xs = (x,)
    return (x[:N_PROMPT].reshape(BATCH, SEQ, D_MODEL), x[N_PROMPT:].reshape(DEC_BATCH, DEC_SEQ, D_MODEL))
```

```python
import functools
import math

import jax
import jax.numpy as jnp
import numpy as np
from jax import lax
from jax.experimental import pallas as pl
from jax.experimental.pallas import tpu as pltpu

F32 = jnp.float32
BF16 = jnp.bfloat16

D_MODEL = 1024
BATCH, SEQ = 4, 4096
DEC_BATCH, DEC_SEQ = 2, 8192
DEPTH = 4
GRID_W = 64
WIN_H, WIN_W = 8, 16
NA_HEADS, NA_HEAD_DIM = 8, 64
DIFF_HEADS, DIFF_QK_DIM, DIFF_V_DIM = 4, 64, 128
SECTION = 512
IN_WIDTH = 6 * SECTION
D_FF = 2816
EPS = 1e-6

N_PROMPT = BATCH * SEQ
N_SAMPLE = DEC_BATCH * DEC_SEQ
N_TOK = N_PROMPT + N_SAMPLE
SEQ_BLOCKS = N_TOK // DEC_SEQ
PROMPT_BLOCKS = N_PROMPT // DEC_SEQ
assert DEC_SEQ == 2 * SEQ and N_PROMPT % DEC_SEQ == 0

LANES = 128
F32_SUBLANES = 8
VMEM_LIMIT_BYTES = 56 * 1024 * 1024

TM = 512
NA_QROWS = 4
NA_BROWS = 12
NA_Q = NA_QROWS * GRID_W
NA_K = NA_BROWS * GRID_W
NA_UNROLL = 4
NEG = -1e30
DIFF_TQ = 512
DIFF_TK = 1024
DIFF_VT_ROWS = DIFF_V_DIM + 16
DIFF_SAFE_SCORE = 50.0
DIFF_NORM_SLACK = 1.02
FF_CHUNK = 256
HALO = F32_SUBLANES
Y_HALO = 16

_NT = (((1,), (1,)), ((), ()))


def _const_spec(shape):
    return pl.BlockSpec(shape, lambda *_: (0,) * len(shape), pipeline_mode=pl.Buffered(1))


def _params(*semantics):
    return pltpu.CompilerParams(dimension_semantics=semantics, vmem_limit_bytes=VMEM_LIMIT_BYTES)


def _rms(x, g):
    return x * lax.rsqrt(jnp.mean(x * x, axis=-1, keepdims=True) + EPS) * g


def _split_heads(x):
    first = lax.broadcasted_iota(jnp.int32, (1, LANES), 1) < LANES // 2
    zero = jnp.zeros_like(x)
    return jnp.concatenate([jnp.where(first, x, zero), jnp.where(first, zero, x)], axis=0)


def _pos_block(i):
    tiles_prompt = N_PROMPT // TM
    return jnp.where(i < tiles_prompt, i % (SEQ // TM), (i - tiles_prompt) % (DEC_SEQ // TM))


def _inproj_kernel(*refs, first_layer):
    if first_layer:
        xp_ref, xs_ref, g_ref, w_ref, cos_ref, sin_lo_ref, sin_hi_ref = refs[:7]
        qa_ref, ka_ref, va_ref, qb_ref, kb_ref, vb_ref, xm_ref = refs[7:]
        x = jnp.where(pl.program_id(0) < N_PROMPT // TM, xp_ref[...], xs_ref[...])
        xm_ref[...] = x
    else:
        x_ref, g_ref, w_ref, cos_ref, sin_lo_ref, sin_hi_ref = refs[:6]
        qa_ref, ka_ref, va_ref, qb_ref, kb_ref, vb_ref = refs[6:]
        x = x_ref[...]
    n = _rms(x, g_ref[...]).astype(BF16)

    def proj(j):
        return jnp.dot(n, w_ref[:, j * SECTION:(j + 1) * SECTION], preferred_element_type=F32)

    qa_ref[...] = (proj(0) * (1.0 / math.sqrt(NA_HEAD_DIM))).astype(BF16)
    ka_ref[...] = proj(1).astype(BF16)
    va_ref[...] = proj(2).astype(BF16)
    cos, sin_lo, sin_hi = cos_ref[...], sin_lo_ref[...], sin_hi_ref[...]

    def rotary(x, scale):
        heads = []
        for h in range(DIFF_HEADS):
            xh = x[:, h * LANES:(h + 1) * LANES]
            r = xh * cos + pltpu.roll(xh, 96, 1) * sin_lo + pltpu.roll(xh, 32, 1) * sin_hi
            heads.append((r * scale).astype(BF16))
        return jnp.concatenate(heads, axis=1)

    qb_ref[...] = rotary(proj(3), 1.0 / math.sqrt(DIFF_QK_DIM))
    kb_ref[...] = rotary(proj(4), 1.0)
    vb_ref[...] = proj(5).astype(BF16)


def _inproj(xs, g, w, cos, sin_lo, sin_hi):
    first_layer = len(xs) == 2
    tiles_prompt = N_PROMPT // TM
    tok = pl.BlockSpec((TM, D_MODEL), lambda i: (i, 0))
    pos = pl.BlockSpec((TM, LANES), lambda i: (_pos_block(i), 0))
    sec = pl.BlockSpec((TM, SECTION), lambda i: (i, 0))
    if first_layer:
        x_specs = [pl.BlockSpec((TM, D_MODEL), lambda i: (jnp.minimum(i, tiles_prompt - 1), 0)),
                   pl.BlockSpec((TM, D_MODEL), lambda i: (jnp.maximum(i - tiles_prompt, 0), 0))]
    else:
        x_specs = [tok]
    return pl.pallas_call(
        functools.partial(_inproj_kernel, first_layer=first_layer),
        grid=(N_TOK // TM,),
        in_specs=x_specs + [_const_spec((1, D_MODEL)), _const_spec((D_MODEL, IN_WIDTH)), pos, pos, pos],
        out_specs=[sec] * 6 + [tok] * first_layer,
        out_shape=([jax.ShapeDtypeStruct((N_TOK, SECTION), BF16)] * 6
                   + [jax.ShapeDtypeStruct((N_TOK, D_MODEL), F32)] * first_layer),
        compiler_params=_params("parallel"),
        name="inproj_first" if first_layer else "inproj",
    )(*xs, g, w, cos, sin_lo, sin_hi)


def _rotary_tables():
    d = DIFF_QK_DIM
    inv_freq = 1.0 / (10000.0 ** (jnp.arange(0, d, 2, dtype=F32) / d))
    ang = jnp.arange(DEC_SEQ, dtype=F32)[:, None] * inv_freq[None, :]
    cos, sin, zero = jnp.cos(ang), jnp.sin(ang), jnp.zeros_like(ang)
    tile = lambda a, b: jnp.concatenate([a, b, a, b], axis=-1)
    return tile(cos, cos), tile(-sin, zero), tile(zero, sin)


def _na_row_patterns(rows):
    nblk = rows // NA_QROWS

    def one(j):
        r0 = j * NA_QROWS
        band = int(np.clip(r0 - WIN_H // 2, 0, rows - NA_BROWS))
        r = r0 + np.arange(NA_QROWS)
        start = np.clip(r - WIN_H // 2, 0, rows - WIN_H)
        kr = band + np.arange(NA_BROWS)
        ok = (kr[None, :] >= start[:, None]) & (kr[None, :] < start[:, None] + WIN_H)
        dr = kr[None, :] - r[:, None] + (WIN_H - 1)
        return ok, np.where(ok, dr, 0)

    pats = [one(0), one(1), one(nblk - 1)]
    for j in range(1, nblk - 1):
        assert all((a == b).all() for a, b in zip(one(j), pats[1]))
    return np.stack([p[0] for p in pats]), np.stack([p[1] for p in pats])


def _na_col_table(rpb_l):
    cols = np.arange(GRID_W)
    col_start = np.clip(cols - WIN_W // 2, 0, GRID_W - WIN_W)
    ok = (cols[None, :] >= col_start[:, None]) & (cols[None, :] < col_start[:, None] + WIN_W)
    dc = cols[None, :] - cols[:, None] + (WIN_W - 1)
    onehot = ((dc[None] == np.arange(2 * WIN_W - 1)[:, None, None]) & ok[None]).astype(np.float32)
    t = jnp.einsum("hrd,dck->hrck", rpb_l.astype(F32), jnp.asarray(onehot), precision=lax.Precision.HIGHEST)
    t = jnp.where(jnp.asarray(ok), t, NEG)
    return jnp.concatenate([t, t], axis=-1)


def _na_kernel(q_ref, k_ref, v_ref, tt_ref, o_ref, bias_ref, s_ref):
    blk = pl.program_id(1)

    @pl.when(blk == 0)
    def _():
        ok, dr = _na_row_patterns(SEQ // GRID_W)
        ok2, dr2 = _na_row_patterns(DEC_SEQ // GRID_W)
        assert (ok == ok2).all() and (dr == dr2).all()
        neg = jnp.full((GRID_W, GRID_W), NEG, F32)
        for pat in range(3):
            for hh in range(2):
                for qr in range(NA_QROWS):
                    r = hh * NA_Q + qr * GRID_W
                    for w in range(NA_BROWS):
                        half = (w % 2) * GRID_W
                        blockval = tt_ref[hh, int(dr[pat, qr, w]), :, half:half + GRID_W] if ok[pat, qr, w] else neg
                        bias_ref[pat, r:r + GRID_W, w * GRID_W:(w + 1) * GRID_W] = blockval

    is_prompt = blk < PROMPT_BLOCKS
    rows = jnp.where(is_prompt, SEQ // GRID_W, DEC_SEQ // GRID_W)
    nblk_p, nblk_s = SEQ // GRID_W // NA_QROWS, DEC_SEQ // GRID_W // NA_QROWS
    first = lax.broadcasted_iota(jnp.int32, (1, LANES), 1) < NA_HEAD_DIM

    def place(j):
        jr = jnp.where(is_prompt, j % nblk_p, j)
        base = jnp.where(is_prompt, (j // nblk_p) * (SEQ // GRID_W), 0)
        last = jnp.where(is_prompt, nblk_p, nblk_s) - 1
        r0 = jr * NA_QROWS
        band = jnp.clip(r0 - WIN_H // 2, 0, rows - NA_BROWS)
        pat = jnp.where(jr == 0, 0, jnp.where(jr == last, 2, 1))
        qoff = pl.multiple_of((base + r0) * GRID_W, NA_Q)
        koff = pl.multiple_of((base + band) * GRID_W, GRID_W)
        return qoff, koff, pat

    def scores(j, slot):
        qoff, koff, pat = place(j)
        qs = _split_heads(q_ref[pl.ds(qoff, NA_Q), :])
        s_ref[slot] = (lax.dot_general(qs, k_ref[pl.ds(koff, NA_K), :], _NT, preferred_element_type=F32)
                       + bias_ref[pat])

    def finish(j, slot):
        qoff, koff, _ = place(j)
        s = s_ref[slot]
        e = jnp.exp(s - jnp.max(s, axis=-1, keepdims=True))
        l = jnp.sum(e, axis=-1, keepdims=True)
        o = jnp.dot(e.astype(BF16), v_ref[pl.ds(koff, NA_K), :], preferred_element_type=F32) / l
        o_ref[pl.ds(qoff, NA_Q), :] = jnp.where(first, o[:NA_Q], o[NA_Q:]).astype(BF16)

    def body(g, carry):
        for u in range(NA_UNROLL):
            j = g * NA_UNROLL + u
            scores(jnp.minimum(j + 1, nblk_s - 1), (u + 1) % 2)
            finish(j, u % 2)
        return carry

    scores(0, 0)
    lax.fori_loop(0, nblk_s // NA_UNROLL, body, 0)


def _na_attention(qa, ka, va, tt):
    blk = pl.BlockSpec((DEC_SEQ, LANES), lambda hp, b: (b, hp))
    return pl.pallas_call(
        _na_kernel,
        grid=(NA_HEADS // 2, SEQ_BLOCKS),
        in_specs=[blk, blk, blk,
                  pl.BlockSpec((2, 2 * WIN_H - 1, GRID_W, LANES), lambda hp, b: (hp, 0, 0, 0))],
        out_specs=blk,
        out_shape=jax.ShapeDtypeStruct((N_TOK, SECTION), BF16),
        scratch_shapes=[pltpu.VMEM((3, 2 * NA_Q, NA_K), F32), pltpu.VMEM((2, 2 * NA_Q, NA_K), F32)],
        compiler_params=_params("arbitrary", "arbitrary"),
        name="na_attention",
    )(qa, ka, va, tt)


def _diff_kernel(lq1_ref, lk1_ref, lq2_ref, lk2_ref, g_ref, q_ref, k_ref, v_ref, o_ref, s_ref, acc_ref, m_ref,
                 vt_ref, *, lam_init):
    is_prompt = pl.program_id(0) < PROMPT_BLOCKS
    n_qblocks = DEC_SEQ // DIFF_TQ

    def transpose_values(c, carry):
        rows = pl.ds(pl.multiple_of(c * DIFF_TQ, DIFF_TQ), DIFF_TQ)
        vt_ref[0:DIFF_V_DIM, rows] = v_ref[rows, :].astype(F32).T.astype(BF16)
        return carry

    lax.fori_loop(0, n_qblocks, transpose_values, 0)
    vt_ref[DIFF_V_DIM:, :] = jnp.ones((DIFF_VT_ROWS - DIFF_V_DIM, DEC_SEQ), BF16)

    def key_rows(qi, c, seq):
        base = (qi // (seq // DIFF_TQ)) * seq
        return pl.ds(pl.multiple_of(base + c * DIFF_TK, DIFF_TK), DIFF_TK)

    def scores(qi, c, slot, seq):
        q = q_ref[pl.ds(pl.multiple_of(qi * DIFF_TQ, DIFF_TQ), DIFF_TQ), :]
        s_ref[slot] = lax.dot_general(k_ref[key_rows(qi, c, seq), :], _split_heads(q), _NT,
                                      preferred_element_type=F32)

    def accumulate(qi, c, slot, seq, shifted):
        s = s_ref[slot]
        p = jnp.exp(s - m_ref[0:1, :] if shifted else s).astype(BF16)
        acc_ref[...] += jnp.dot(vt_ref[:, key_rows(qi, c, seq)], p, preferred_element_type=F32)

    def finish(qi):
        acc = acc_ref[...]
        sm1 = acc[:DIFF_V_DIM, :DIFF_TQ] / acc[DIFF_V_DIM:DIFF_V_DIM + 1, :DIFF_TQ]
        sm2 = acc[:DIFF_V_DIM, DIFF_TQ:] / acc[DIFF_V_DIM:DIFF_V_DIM + 1, DIFF_TQ:]
        lam = (jnp.exp(jnp.sum(lq1_ref[...] * lk1_ref[...], axis=-1, keepdims=True))
               - jnp.exp(jnp.sum(lq2_ref[...] * lk2_ref[...], axis=-1, keepdims=True)) + lam_init)
        d = sm1 - lam * sm2
        d = d * lax.rsqrt(jnp.mean(d * d, axis=0, keepdims=True) + EPS)
        y = d.T * g_ref[...] * (1.0 - lam_init)
        o_ref[pl.ds(pl.multiple_of(qi * DIFF_TQ, DIFF_TQ), DIFF_TQ), :] = y.astype(BF16)
        acc_ref[...] = jnp.zeros_like(acc_ref)

    lane_half = lambda axis: lax.broadcasted_iota(jnp.int32, (LANES, LANES), axis) // DIFF_QK_DIM
    same_half = jnp.where(lane_half(0) == lane_half(1), 1.0, 0.0).astype(BF16)

    def max_sq_norms(ref):
        x = ref[...]
        return jnp.max(jnp.dot(x * x, same_half, preferred_element_type=F32), axis=0, keepdims=True)

    bound_sq = jnp.max(max_sq_norms(q_ref) * max_sq_norms(k_ref)) * DIFF_NORM_SLACK
    needs_max = bound_sq > DIFF_SAFE_SCORE ** 2
    acc_ref[...] = jnp.zeros_like(acc_ref)

    def unshifted(seq):
        nchunks = seq // DIFF_TK
        assert nchunks % 2 == 0

        def body(qi, carry):
            for c in range(nchunks):
                if c + 1 < nchunks:
                    scores(qi, c + 1, (c + 1) % 2, seq)
                else:
                    scores(jnp.minimum(qi + 1, n_qblocks - 1), 0, 0, seq)
                accumulate(qi, c, c % 2, seq, False)
            finish(qi)
            return carry

        def run():
            scores(0, 0, 0, seq)
            lax.fori_loop(0, n_qblocks, body, 0)
        return run

    def shifted(seq):
        nchunks = seq // DIFF_TK

        def qblock(qi, carry):
            def col_max(c, m):
                scores(qi, c, 0, seq)
                return jnp.maximum(m, jnp.max(s_ref[0], axis=0, keepdims=True))

            m = lax.fori_loop(0, nchunks, col_max, jnp.full((1, 2 * DIFF_TQ), NEG, F32))
            m_ref[...] = jnp.broadcast_to(m, m_ref.shape)

            def chunk(c, carry):
                scores(qi, c, 0, seq)
                accumulate(qi, c, 0, seq, True)
                return carry

            lax.fori_loop(0, nchunks, chunk, 0)
            finish(qi)
            return carry

        def run():
            lax.fori_loop(0, n_qblocks, qblock, 0)
        return run

    lax.cond(needs_max,
             lambda: lax.cond(is_prompt, shifted(SEQ), shifted(DEC_SEQ)),
             lambda: lax.cond(is_prompt, unshifted(SEQ), unshifted(DEC_SEQ)))


def _diff_attention(lams, g, qb, kb, vb, *, lam_init):
    vec = _const_spec((1, DIFF_QK_DIM))
    blk = pl.BlockSpec((DEC_SEQ, LANES), lambda b, h: (b, h))
    return pl.pallas_call(
        functools.partial(_diff_kernel, lam_init=lam_init),
        grid=(SEQ_BLOCKS, DIFF_HEADS),
        in_specs=[vec, vec, vec, vec, _const_spec((1, DIFF_V_DIM)), blk, blk, blk],
        out_specs=blk,
        out_shape=jax.ShapeDtypeStruct((N_TOK, SECTION), BF16),
        scratch_shapes=[pltpu.VMEM((2, DIFF_TK, 2 * DIFF_TQ), F32), pltpu.VMEM((DIFF_VT_ROWS, 2 * DIFF_TQ), F32),
                        pltpu.VMEM((F32_SUBLANES, 2 * DIFF_TQ), F32), pltpu.VMEM((DIFF_VT_ROWS, DEC_SEQ), BF16)],
        compiler_params=_params("parallel", "parallel"),
        name="diff_attention",
    )(*lams, g, qb, kb, vb)


def _ffn_kernel(x_ref, xp_ref, xn_ref, ya_ref, yap_ref, yan_ref, yb_ref, ybp_ref, ybn_ref, wo_ref, g_ref, wup_ref,
                cw_ref, cb_ref, wdn_ref, gf_ref, *out_and_scratch, final):
    h_ref = out_and_scratch[-1]
    i = pl.program_id(0)
    tiles_prompt = N_PROMPT // TM
    per_seq = jnp.where(i < tiles_prompt, SEQ // TM, DEC_SEQ // TM)
    has_prev = (i % per_seq) != 0
    has_next = (i % per_seq) != per_seq - 1

    ext = TM + 2 * HALO
    ya = jnp.concatenate([yap_ref[...], ya_ref[...], yan_ref[...]], axis=0)
    yb = jnp.concatenate([ybp_ref[...], yb_ref[...], ybn_ref[...]], axis=0)
    attn = (jnp.dot(ya, wo_ref[:SECTION, :], preferred_element_type=F32)
            + jnp.dot(yb, wo_ref[SECTION:, :], preferred_element_type=F32))
    xe = (jnp.concatenate([xp_ref[...], x_ref[...], xn_ref[...]], axis=0)
          + attn[Y_HALO - HALO:Y_HALO - HALO + ext])
    x = xe[HALO:HALO + TM]
    n = _rms(xe, g_ref[...]).astype(BF16)
    row = lax.broadcasted_iota(jnp.int32, (ext, 1), 0)
    keep = ((row >= HALO) | has_prev) & ((row < HALO + TM) | has_next)

    for c in range(D_FF // FF_CHUNK):
        lo = c * FF_CHUNK
        gate = jnp.dot(n, wup_ref[:, lo:lo + FF_CHUNK], preferred_element_type=F32)
        val = jnp.dot(n, wup_ref[:, D_FF + lo:D_FF + lo + FF_CHUNK], preferred_element_type=F32)
        gate = jnp.where(keep, gate, 0.0)
        cw = cw_ref[:, lo:lo + FF_CHUNK]
        conv = (pltpu.roll(gate, 1, 0) * cw[0:1] + gate * cw[1:2] + pltpu.roll(gate, ext - 1, 0) * cw[2:3]
                + cb_ref[:, lo:lo + FF_CHUNK])
        act = 0.5 * conv * (1.0 + lax.erf(conv * (1.0 / math.sqrt(2.0))))
        h_ref[:, lo:lo + FF_CHUNK] = (act * val)[HALO:HALO + TM].astype(BF16)

    y = x + jnp.dot(h_ref[...], wdn_ref[...], preferred_element_type=F32)
    if final:
        prompt_ref, sample_ref = out_and_scratch[:2]
        res = _rms(y, gf_ref[...])

        @pl.when(i < tiles_prompt)
        def _():
            prompt_ref[...] = res

        @pl.when(i >= tiles_prompt)
        def _():
            sample_ref[...] = res
    else:
        out_and_scratch[0][...] = y


def _ffn(x, ya, yb, wo, g, wup, cw, cb, wdn, gf, *, final):
    def tile_and_halos(width, halo):
        per_tile = TM // halo
        return [pl.BlockSpec((TM, width), lambda i: (i, 0)),
                pl.BlockSpec((halo, width), lambda i: (jnp.maximum(i * per_tile - 1, 0), 0)),
                pl.BlockSpec((halo, width), lambda i: (jnp.minimum((i + 1) * per_tile, N_TOK // halo - 1), 0))]

    x_specs = tile_and_halos(D_MODEL, HALO)
    y_specs = tile_and_halos(SECTION, Y_HALO)
    tiles_prompt = N_PROMPT // TM
    if final:
        out_specs = [pl.BlockSpec((TM, D_MODEL), lambda i: (jnp.minimum(i, tiles_prompt - 1), 0)),
                     pl.BlockSpec((TM, D_MODEL), lambda i: (jnp.maximum(i - tiles_prompt, 0), 0))]
        out_shape = [jax.ShapeDtypeStruct((N_PROMPT, D_MODEL), F32), jax.ShapeDtypeStruct((N_SAMPLE, D_MODEL), F32)]
    else:
        out_specs, out_shape = x_specs[0], jax.ShapeDtypeStruct((N_TOK, D_MODEL), F32)
    return pl.pallas_call(
        functools.partial(_ffn_kernel, final=final),
        grid=(N_TOK // TM,),
        in_specs=x_specs + y_specs + y_specs + [
            _const_spec((2 * SECTION, D_MODEL)), _const_spec((1, D_MODEL)), _const_spec((D_MODEL, 2 * D_FF)),
            _const_spec((3, D_FF)), _const_spec((1, D_FF)), _const_spec((D_FF, D_MODEL)), _const_spec((1, D_MODEL))],
        out_specs=out_specs,
        out_shape=out_shape,
        scratch_shapes=[pltpu.VMEM((TM, D_FF), BF16)],
        compiler_params=_params("arbitrary"),
        name="ffn_final" if final else "ffn",
    )(x, x, x, ya, ya, ya, yb, yb, yb, wo, g, wup, cw, cb, wdn, gf)


def kernel(x_prompt, x_sample, g_attn, w_in, rpb, lam_q1, lam_k1, lam_q2, lam_k2, subln_g, w_out, g_ffn, w_up,
           conv_w, conv_b, w_down, g_final):
    cos, sin_lo, sin_hi = _rotary_tables()
    row = lambda v: v.reshape(1, -1).astype(F32)
    xs = (x_prompt.reshape(N_PROMPT, D_MODEL), x_sample.reshape(N_SAMPLE, D_MODEL))
    for l in range(DEPTH):
        lam_init = 0.8 - 0.6 * math.exp(-0.3 * l)
        qa, ka, va, qb, kb, vb, *merged = _inproj(xs, row(g_attn[l]), w_in[l].astype(BF16), cos, sin_lo, sin_hi)
        x = merged[0] if merged else xs[0]
        ya = _na_attention(qa, ka, va, _na_col_table(rpb[l]))
        lams = (row(lam_q1[l]), row(lam_k1[l]), row(lam_q2[l]), row(lam_k2[l]))
        yb = _diff_attention(lams, row(subln_g[l]), qb, kb, vb, lam_init=lam_init)
        x = _ffn(x, ya, yb, w_out[l].astype(BF16), row(g_ffn[l]), w_up[l].astype(BF16), conv_w[l].astype(F32),
                 row(conv_b[l]), w_down[l].astype(BF16), row(g_final), final=(l == DEPTH - 1))
        xs = (x,)
    y_prompt, y_sample = x
    return (y_prompt.reshape(BATCH, SEQ, D_MODEL), y_sample.reshape(DEC_BATCH, DEC_SEQ, D_MODEL))
```

```python
import functools
import math

import jax
import jax.numpy as jnp
import numpy as np
from jax import lax
from jax.experimental import pallas as pl
from jax.experimental.pallas import tpu as pltpu

F32 = jnp.float32
BF16 = jnp.bfloat16

D_MODEL = 1024
BATCH, SEQ = 4, 4096
DEC_BATCH, DEC_SEQ = 2, 8192
DEPTH = 4
GRID_W = 64
WIN_H, WIN_W = 8, 16
NA_HEADS, NA_HEAD_DIM = 8, 64
DIFF_HEADS, DIFF_QK_DIM, DIFF_V_DIM = 4, 64, 128
SECTION = 512
IN_WIDTH = 6 * SECTION
D_FF = 2816
EPS = 1e-6

N_PROMPT = BATCH * SEQ
N_SAMPLE = DEC_BATCH * DEC_SEQ
N_TOK = N_PROMPT + N_SAMPLE
SEQ_BLOCKS = N_TOK // DEC_SEQ
PROMPT_BLOCKS = N_PROMPT // DEC_SEQ
assert DEC_SEQ == 2 * SEQ and N_PROMPT % DEC_SEQ == 0

LANES = 128
F32_SUBLANES = 8
VMEM_LIMIT_BYTES = 56 * 1024 * 1024

TM = 512
NA_QROWS = 4
NA_BROWS = 12
NA_Q = NA_QROWS * GRID_W
NA_K = NA_BROWS * GRID_W
NA_UNROLL = 4
NEG = -1e30
DIFF_TQ = 512
DIFF_TK = 1024
DIFF_VT_ROWS = DIFF_V_DIM + 16
DIFF_SAFE_SCORE = 50.0
DIFF_NORM_SLACK = 1.02
FF_CHUNK = 256
HALO = F32_SUBLANES
Y_HALO = 16

_NT = (((1,), (1,)), ((), ()))


def _const_spec(shape):
    return pl.BlockSpec(shape, lambda *_: (0,) * len(shape), pipeline_mode=pl.Buffered(1))


def _layer_spec(layer, shape):
    return pl.BlockSpec((None,) + shape, lambda *_: (layer,) + (0,) * len(shape), pipeline_mode=pl.Buffered(1))


def _params(*semantics):
    return pltpu.CompilerParams(dimension_semantics=semantics, vmem_limit_bytes=VMEM_LIMIT_BYTES)


def _rms(x, g):
    return x * lax.rsqrt(jnp.mean(x * x, axis=-1, keepdims=True) + EPS) * g


def _split_heads(x):
    first = lax.broadcasted_iota(jnp.int32, (1, LANES), 1) < LANES // 2
    zero = jnp.zeros_like(x)
    return jnp.concatenate([jnp.where(first, x, zero), jnp.where(first, zero, x)], axis=0)


def _pos_block(i):
    tiles_prompt = N_PROMPT // TM
    return jnp.where(i < tiles_prompt, i % (SEQ // TM), (i - tiles_prompt) % (DEC_SEQ // TM))


def _inproj_kernel(*refs, first_layer):
    if first_layer:
        xp_ref, xs_ref, g_ref, w_ref, cos_ref, sin_lo_ref, sin_hi_ref = refs[:7]
        qa_ref, ka_ref, va_ref, qb_ref, kb_ref, vb_ref, xm_ref = refs[7:]
        x = jnp.where(pl.program_id(0) < N_PROMPT // TM, xp_ref[...], xs_ref[...])
        xm_ref[...] = x
    else:
        x_ref, g_ref, w_ref, cos_ref, sin_lo_ref, sin_hi_ref = refs[:6]
        qa_ref, ka_ref, va_ref, qb_ref, kb_ref, vb_ref = refs[6:]
        x = x_ref[...]
    n = _rms(x, g_ref[...]).astype(BF16)

    def proj(j):
        return jnp.dot(n, w_ref[:, j * SECTION:(j + 1) * SECTION], preferred_element_type=F32)

    qa_ref[...] = (proj(0) * (1.0 / math.sqrt(NA_HEAD_DIM))).astype(BF16)
    ka_ref[...] = proj(1).astype(BF16)
    va_ref[...] = proj(2).astype(BF16)
    cos, sin_lo, sin_hi = cos_ref[...], sin_lo_ref[...], sin_hi_ref[...]

    def rotary(x, scale):
        heads = []
        for h in range(DIFF_HEADS):
            xh = x[:, h * LANES:(h + 1) * LANES]
            r = xh * cos + pltpu.roll(xh, 96, 1) * sin_lo + pltpu.roll(xh, 32, 1) * sin_hi
            heads.append((r * scale).astype(BF16))
        return jnp.concatenate(heads, axis=1)

    qb_ref[...] = rotary(proj(3), 1.0 / math.sqrt(DIFF_QK_DIM))
    kb_ref[...] = rotary(proj(4), 1.0)
    vb_ref[...] = proj(5).astype(BF16)


def _inproj(xs, g, w, layer, cos, sin_lo, sin_hi):
    first_layer = len(xs) == 2
    tiles_prompt = N_PROMPT // TM
    tok = pl.BlockSpec((TM, D_MODEL), lambda i: (i, 0))
    pos = pl.BlockSpec((TM, LANES), lambda i: (_pos_block(i), 0))
    sec = pl.BlockSpec((TM, SECTION), lambda i: (i, 0))
    if first_layer:
        x_specs = [pl.BlockSpec((TM, D_MODEL), lambda i: (jnp.minimum(i, tiles_prompt - 1), 0)),
                   pl.BlockSpec((TM, D_MODEL), lambda i: (jnp.maximum(i - tiles_prompt, 0), 0))]
    else:
        x_specs = [tok]
    return pl.pallas_call(
        functools.partial(_inproj_kernel, first_layer=first_layer),
        grid=(N_TOK // TM,),
        in_specs=x_specs + [_const_spec((1, D_MODEL)), _layer_spec(layer, (D_MODEL, IN_WIDTH)), pos, pos, pos],
        out_specs=[sec] * 6 + [tok] * first_layer,
        out_shape=([jax.ShapeDtypeStruct((N_TOK, SECTION), BF16)] * 6
                   + [jax.ShapeDtypeStruct((N_TOK, D_MODEL), F32)] * first_layer),
        compiler_params=_params("parallel"),
        name="inproj_first" if first_layer else "inproj",
    )(*xs, g, w, cos, sin_lo, sin_hi)


def _rotary_tables():
    d = DIFF_QK_DIM
    inv_freq = 1.0 / (10000.0 ** (jnp.arange(0, d, 2, dtype=F32) / d))
    ang = jnp.arange(DEC_SEQ, dtype=F32)[:, None] * inv_freq[None, :]
    cos, sin, zero = jnp.cos(ang), jnp.sin(ang), jnp.zeros_like(ang)
    tile = lambda a, b: jnp.concatenate([a, b, a, b], axis=-1)
    return tile(cos, cos), tile(-sin, zero), tile(zero, sin)


def _na_row_patterns(rows):
    nblk = rows // NA_QROWS

    def one(j):
        r0 = j * NA_QROWS
        band = int(np.clip(r0 - WIN_H // 2, 0, rows - NA_BROWS))
        r = r0 + np.arange(NA_QROWS)
        start = np.clip(r - WIN_H // 2, 0, rows - WIN_H)
        kr = band + np.arange(NA_BROWS)
        ok = (kr[None, :] >= start[:, None]) & (kr[None, :] < start[:, None] + WIN_H)
        dr = kr[None, :] - r[:, None] + (WIN_H - 1)
        return ok, np.where(ok, dr, 0)

    pats = [one(0), one(1), one(nblk - 1)]
    for j in range(1, nblk - 1):
        assert all((a == b).all() for a, b in zip(one(j), pats[1]))
    return np.stack([p[0] for p in pats]), np.stack([p[1] for p in pats])


def _na_col_table(rpb_l):
    cols = np.arange(GRID_W)
    col_start = np.clip(cols - WIN_W // 2, 0, GRID_W - WIN_W)
    ok = (cols[None, :] >= col_start[:, None]) & (cols[None, :] < col_start[:, None] + WIN_W)
    dc = cols[None, :] - cols[:, None] + (WIN_W - 1)
    onehot = ((dc[None] == np.arange(2 * WIN_W - 1)[:, None, None]) & ok[None]).astype(np.float32)
    t = jnp.einsum("hrd,dck->hrck", rpb_l.astype(F32), jnp.asarray(onehot), precision=lax.Precision.HIGHEST)
    t = jnp.where(jnp.asarray(ok), t, NEG)
    return jnp.concatenate([t, t], axis=-1)


def _na_kernel(q_ref, k_ref, v_ref, tt_ref, o_ref, bias_ref, s_ref):
    blk = pl.program_id(1)

    @pl.when(blk == 0)
    def _():
        ok, dr = _na_row_patterns(SEQ // GRID_W)
        ok2, dr2 = _na_row_patterns(DEC_SEQ // GRID_W)
        assert (ok == ok2).all() and (dr == dr2).all()
        neg = jnp.full((GRID_W, GRID_W), NEG, F32)
        for pat in range(3):
            for hh in range(2):
                for qr in range(NA_QROWS):
                    r = hh * NA_Q + qr * GRID_W
                    for w in range(NA_BROWS):
                        half = (w % 2) * GRID_W
                        blockval = tt_ref[hh, int(dr[pat, qr, w]), :, half:half + GRID_W] if ok[pat, qr, w] else neg
                        bias_ref[pat, r:r + GRID_W, w * GRID_W:(w + 1) * GRID_W] = blockval

    is_prompt = blk < PROMPT_BLOCKS
    rows = jnp.where(is_prompt, SEQ // GRID_W, DEC_SEQ // GRID_W)
    nblk_p, nblk_s = SEQ // GRID_W // NA_QROWS, DEC_SEQ // GRID_W // NA_QROWS
    first = lax.broadcasted_iota(jnp.int32, (1, LANES), 1) < NA_HEAD_DIM

    def place(j):
        jr = jnp.where(is_prompt, j % nblk_p, j)
        base = jnp.where(is_prompt, (j // nblk_p) * (SEQ // GRID_W), 0)
        last = jnp.where(is_prompt, nblk_p, nblk_s) - 1
        r0 = jr * NA_QROWS
        band = jnp.clip(r0 - WIN_H // 2, 0, rows - NA_BROWS)
        pat = jnp.where(jr == 0, 0, jnp.where(jr == last, 2, 1))
        qoff = pl.multiple_of((base + r0) * GRID_W, NA_Q)
        koff = pl.multiple_of((base + band) * GRID_W, GRID_W)
        return qoff, koff, pat

    def scores(j, slot):
        qoff, koff, pat = place(j)
        qs = _split_heads(q_ref[pl.ds(qoff, NA_Q), :])
        s_ref[slot] = (lax.dot_general(qs, k_ref[pl.ds(koff, NA_K), :], _NT, preferred_element_type=F32)
                       + bias_ref[pat])

    def finish(j, slot):
        qoff, koff, _ = place(j)
        s = s_ref[slot]
        e = jnp.exp(s - jnp.max(s, axis=-1, keepdims=True))
        l = jnp.sum(e, axis=-1, keepdims=True)
        o = jnp.dot(e.astype(BF16), v_ref[pl.ds(koff, NA_K), :], preferred_element_type=F32) / l
        o_ref[pl.ds(qoff, NA_Q), :] = jnp.where(first, o[:NA_Q], o[NA_Q:]).astype(BF16)

    def body(g, carry):
        for u in range(NA_UNROLL):
            j = g * NA_UNROLL + u
            scores(jnp.minimum(j + 1, nblk_s - 1), (u + 1) % 2)
            finish(j, u % 2)
        return carry

    scores(0, 0)
    lax.fori_loop(0, nblk_s // NA_UNROLL, body, 0)


def _na_attention(qa, ka, va, tt):
    blk = pl.BlockSpec((DEC_SEQ, LANES), lambda hp, b: (b, hp))
    return pl.pallas_call(
        _na_kernel,
        grid=(NA_HEADS // 2, SEQ_BLOCKS),
        in_specs=[blk, blk, blk,
                  pl.BlockSpec((2, 2 * WIN_H - 1, GRID_W, LANES), lambda hp, b: (hp, 0, 0, 0))],
        out_specs=blk,
        out_shape=jax.ShapeDtypeStruct((N_TOK, SECTION), BF16),
        scratch_shapes=[pltpu.VMEM((3, 2 * NA_Q, NA_K), F32), pltpu.VMEM((2, 2 * NA_Q, NA_K), F32)],
        compiler_params=_params("arbitrary", "arbitrary"),
        name="na_attention",
    )(qa, ka, va, tt)


def _diff_kernel(lq1_ref, lk1_ref, lq2_ref, lk2_ref, g_ref, q_ref, k_ref, v_ref, o_ref, s_ref, acc_ref, m_ref,
                 vt_ref, *, lam_init):
    is_prompt = pl.program_id(0) < PROMPT_BLOCKS
    n_qblocks = DEC_SEQ // DIFF_TQ

    def transpose_values(c, carry):
        rows = pl.ds(pl.multiple_of(c * DIFF_TQ, DIFF_TQ), DIFF_TQ)
        vt_ref[0:DIFF_V_DIM, rows] = v_ref[rows, :].astype(F32).T.astype(BF16)
        return carry

    lax.fori_loop(0, n_qblocks, transpose_values, 0)
    vt_ref[DIFF_V_DIM:, :] = jnp.ones((DIFF_VT_ROWS - DIFF_V_DIM, DEC_SEQ), BF16)

    def key_rows(qi, c, seq):
        base = (qi // (seq // DIFF_TQ)) * seq
        return pl.ds(pl.multiple_of(base + c * DIFF_TK, DIFF_TK), DIFF_TK)

    def scores(qi, c, slot, seq):
        q = q_ref[pl.ds(pl.multiple_of(qi * DIFF_TQ, DIFF_TQ), DIFF_TQ), :]
        s_ref[slot] = lax.dot_general(k_ref[key_rows(qi, c, seq), :], _split_heads(q), _NT,
                                      preferred_element_type=F32)

    def accumulate(qi, c, slot, seq, shifted):
        s = s_ref[slot]
        p = jnp.exp(s - m_ref[0:1, :] if shifted else s).astype(BF16)
        acc_ref[...] += jnp.dot(vt_ref[:, key_rows(qi, c, seq)], p, preferred_element_type=F32)

    def finish(qi):
        acc = acc_ref[...]
        sm1 = acc[:DIFF_V_DIM, :DIFF_TQ] / acc[DIFF_V_DIM:DIFF_V_DIM + 1, :DIFF_TQ]
        sm2 = acc[:DIFF_V_DIM, DIFF_TQ:] / acc[DIFF_V_DIM:DIFF_V_DIM + 1, DIFF_TQ:]
        lam = (jnp.exp(jnp.sum(lq1_ref[...] * lk1_ref[...], axis=-1, keepdims=True))
               - jnp.exp(jnp.sum(lq2_ref[...] * lk2_ref[...], axis=-1, keepdims=True)) + lam_init)
        d = sm1 - lam * sm2
        d = d * lax.rsqrt(jnp.mean(d * d, axis=0, keepdims=True) + EPS)
        y = d.T * g_ref[...] * (1.0 - lam_init)
        o_ref[pl.ds(pl.multiple_of(qi * DIFF_TQ, DIFF_TQ), DIFF_TQ), :] = y.astype(BF16)
        acc_ref[...] = jnp.zeros_like(acc_ref)

    lane_half = lambda axis: lax.broadcasted_iota(jnp.int32, (LANES, LANES), axis) // DIFF_QK_DIM
    same_half = jnp.where(lane_half(0) == lane_half(1), 1.0, 0.0).astype(BF16)

    def max_sq_norms(ref):
        x = ref[...]
        return jnp.max(jnp.dot(x * x, same_half, preferred_element_type=F32), axis=0, keepdims=True)

    bound_sq = jnp.max(max_sq_norms(q_ref) * max_sq_norms(k_ref)) * DIFF_NORM_SLACK
    needs_max = bound_sq > DIFF_SAFE_SCORE ** 2
    acc_ref[...] = jnp.zeros_like(acc_ref)

    def unshifted(seq):
        nchunks = seq // DIFF_TK
        assert nchunks % 2 == 0

        def body(qi, carry):
            for c in range(nchunks):
                if c + 1 < nchunks:
                    scores(qi, c + 1, (c + 1) % 2, seq)
                else:
                    scores(jnp.minimum(qi + 1, n_qblocks - 1), 0, 0, seq)
                accumulate(qi, c, c % 2, seq, False)
            finish(qi)
            return carry

        def run():
            scores(0, 0, 0, seq)
            lax.fori_loop(0, n_qblocks, body, 0)
        return run

    def shifted(seq):
        nchunks = seq // DIFF_TK

        def qblock(qi, carry):
            def col_max(c, m):
                scores(qi, c, 0, seq)
                return jnp.maximum(m, jnp.max(s_ref[0], axis=0, keepdims=True))

            m = lax.fori_loop(0, nchunks, col_max, jnp.full((1, 2 * DIFF_TQ), NEG, F32))
            m_ref[...] = jnp.broadcast_to(m, m_ref.shape)

            def chunk(c, carry):
                scores(qi, c, 0, seq)
                accumulate(qi, c, 0, seq, True)
                return carry

            lax.fori_loop(0, nchunks, chunk, 0)
            finish(qi)
            return carry

        def run():
            lax.fori_loop(0, n_qblocks, qblock, 0)
        return run

    lax.cond(needs_max,
             lambda: lax.cond(is_prompt, shifted(SEQ), shifted(DEC_SEQ)),
             lambda: lax.cond(is_prompt, unshifted(SEQ), unshifted(DEC_SEQ)))


def _diff_attention(lams, g, qb, kb, vb, *, lam_init):
    vec = _const_spec((1, DIFF_QK_DIM))
    blk = pl.BlockSpec((DEC_SEQ, LANES), lambda b, h: (b, h))
    return pl.pallas_call(
        functools.partial(_diff_kernel, lam_init=lam_init),
        grid=(SEQ_BLOCKS, DIFF_HEADS),
        in_specs=[vec, vec, vec, vec, _const_spec((1, DIFF_V_DIM)), blk, blk, blk],
        out_specs=blk,
        out_shape=jax.ShapeDtypeStruct((N_TOK, SECTION), BF16),
        scratch_shapes=[pltpu.VMEM((2, DIFF_TK, 2 * DIFF_TQ), F32), pltpu.VMEM((DIFF_VT_ROWS, 2 * DIFF_TQ), F32),
                        pltpu.VMEM((F32_SUBLANES, 2 * DIFF_TQ), F32), pltpu.VMEM((DIFF_VT_ROWS, DEC_SEQ), BF16)],
        compiler_params=_params("parallel", "parallel"),
        name="diff_attention",
    )(*lams, g, qb, kb, vb)


def _ffn_kernel(x_ref, xp_ref, xn_ref, ya_ref, yap_ref, yan_ref, yb_ref, ybp_ref, ybn_ref, wo_ref, g_ref, wup_ref,
                cw_ref, cb_ref, wdn_ref, gf_ref, *out_and_scratch, final):
    h_ref = out_and_scratch[-1]
    i = pl.program_id(0)
    tiles_prompt = N_PROMPT // TM
    per_seq = jnp.where(i < tiles_prompt, SEQ // TM, DEC_SEQ // TM)
    has_prev = (i % per_seq) != 0
    has_next = (i % per_seq) != per_seq - 1

    ext = TM + 2 * HALO
    ya = jnp.concatenate([yap_ref[...], ya_ref[...], yan_ref[...]], axis=0)
    yb = jnp.concatenate([ybp_ref[...], yb_ref[...], ybn_ref[...]], axis=0)
    attn = (jnp.dot(ya, wo_ref[:SECTION, :], preferred_element_type=F32)
            + jnp.dot(yb, wo_ref[SECTION:, :], preferred_element_type=F32))
    xe = (jnp.concatenate([xp_ref[...], x_ref[...], xn_ref[...]], axis=0)
          + attn[Y_HALO - HALO:Y_HALO - HALO + ext])
    x = xe[HALO:HALO + TM]
    n = _rms(xe, g_ref[...]).astype(BF16)
    row = lax.broadcasted_iota(jnp.int32, (ext, 1), 0)
    keep = ((row >= HALO) | has_prev) & ((row < HALO + TM) | has_next)

    for c in range(D_FF // FF_CHUNK):
        lo = c * FF_CHUNK
        gate = jnp.dot(n, wup_ref[:, lo:lo + FF_CHUNK], preferred_element_type=F32)
        val = jnp.dot(n, wup_ref[:, D_FF + lo:D_FF + lo + FF_CHUNK], preferred_element_type=F32)
        gate = jnp.where(keep, gate, 0.0)
        cw = cw_ref[:, lo:lo + FF_CHUNK]
        conv = (pltpu.roll(gate, 1, 0) * cw[0:1] + gate * cw[1:2] + pltpu.roll(gate, ext - 1, 0) * cw[2:3]
                + cb_ref[:, lo:lo + FF_CHUNK])
        act = 0.5 * conv * (1.0 + lax.erf(conv * (1.0 / math.sqrt(2.0))))
        h_ref[:, lo:lo + FF_CHUNK] = (act * val)[HALO:HALO + TM].astype(BF16)

    y = x + jnp.dot(h_ref[...], wdn_ref[...], preferred_element_type=F32)
    if final:
        prompt_ref, sample_ref = out_and_scratch[:2]
        res = _rms(y, gf_ref[...])

        @pl.when(i < tiles_prompt)
        def _():
            prompt_ref[...] = res

        @pl.when(i >= tiles_prompt)
        def _():
            sample_ref[...] = res
    else:
        out_and_scratch[0][...] = y


def _ffn(x, ya, yb, wo, g, wup, cw, cb, wdn, gf, *, layer, final):
    def tile_and_halos(width, halo):
        per_tile = TM // halo
        return [pl.BlockSpec((TM, width), lambda i: (i, 0)),
                pl.BlockSpec((halo, width), lambda i: (jnp.maximum(i * per_tile - 1, 0), 0)),
                pl.BlockSpec((halo, width), lambda i: (jnp.minimum((i + 1) * per_tile, N_TOK // halo - 1), 0))]

    x_specs = tile_and_halos(D_MODEL, HALO)
    y_specs = tile_and_halos(SECTION, Y_HALO)
    tiles_prompt = N_PROMPT // TM
    if final:
        out_specs = [pl.BlockSpec((TM, D_MODEL), lambda i: (jnp.minimum(i, tiles_prompt - 1), 0)),
                     pl.BlockSpec((TM, D_MODEL), lambda i: (jnp.maximum(i - tiles_prompt, 0), 0))]
        out_shape = [jax.ShapeDtypeStruct((N_PROMPT, D_MODEL), F32), jax.ShapeDtypeStruct((N_SAMPLE, D_MODEL), F32)]
    else:
        out_specs, out_shape = x_specs[0], jax.ShapeDtypeStruct((N_TOK, D_MODEL), F32)
    return pl.pallas_call(
        functools.partial(_ffn_kernel, final=final),
        grid=(N_TOK // TM,),
        in_specs=x_specs + y_specs + y_specs + [
            _layer_spec(layer, (2 * SECTION, D_MODEL)), _const_spec((1, D_MODEL)),
            _layer_spec(layer, (D_MODEL, 2 * D_FF)), _const_spec((3, D_FF)), _const_spec((1, D_FF)),
            _layer_spec(layer, (D_FF, D_MODEL)), _const_spec((1, D_MODEL))],
        out_specs=out_specs,
        out_shape=out_shape,
        scratch_shapes=[pltpu.VMEM((TM, D_FF), BF16)],
        compiler_params=_params("arbitrary"),
        name="ffn_final" if final else "ffn",
    )(x, x, x, ya, ya, ya, yb, yb, yb, wo, g, wup, cw, cb, wdn, gf)


def kernel(x_prompt, x_sample, g_attn, w_in, rpb, lam_q1, lam_k1, lam_q2, lam_k2, subln_g, w_out, g_ffn, w_up,
           conv_w, conv_b, w_down, g_final):
    cos, sin_lo, sin_hi = _rotary_tables()
    row = lambda v: v.reshape(1, -1).astype(F32)
    xs = (x_prompt.reshape(N_PROMPT, D_MODEL), x_sample.reshape(N_SAMPLE, D_MODEL))
    w_in, w_out, w_up, w_down = (w.astype(BF16) for w in (w_in, w_out, w_up, w_down))
    for l in range(DEPTH):
        lam_init = 0.8 - 0.6 * math.exp(-0.3 * l)
        qa, ka, va, qb, kb, vb, *merged = _inproj(xs, row(g_attn[l]), w_in, l, cos, sin_lo, sin_hi)
        x = merged[0] if merged else xs[0]
        ya = _na_attention(qa, ka, va, _na_col_table(rpb[l]))
        lams = (row(lam_q1[l]), row(lam_k1[l]), row(lam_q2[l]), row(lam_k2[l]))
        yb = _diff_attention(lams, row(subln_g[l]), qb, kb, vb, lam_init=lam_init)
        x = _ffn(x, ya, yb, w_out, row(g_ffn[l]), w_up, conv_w[l].astype(F32), row(conv_b[l]), w_down, row(g_final),
                 layer=l, final=(l == DEPTH - 1))
        xs = (x,)
    y_prompt, y_sample = x
    return (y_prompt.reshape(BATCH, SEQ, D_MODEL), y_sample.reshape(DEC_BATCH, DEC_SEQ, D_MODEL))
```

```python
import functools
import math

import jax
import jax.numpy as jnp
import numpy as np
from jax import lax
from jax.experimental import pallas as pl
from jax.experimental.pallas import tpu as pltpu

F32 = jnp.float32
BF16 = jnp.bfloat16

D_MODEL = 1024
BATCH, SEQ = 4, 4096
DEC_BATCH, DEC_SEQ = 2, 8192
DEPTH = 4
GRID_W = 64
WIN_H, WIN_W = 8, 16
NA_HEADS, NA_HEAD_DIM = 8, 64
DIFF_HEADS, DIFF_QK_DIM, DIFF_V_DIM = 4, 64, 128
SECTION = 512
IN_WIDTH = 6 * SECTION
D_FF = 2816
EPS = 1e-6

N_PROMPT = BATCH * SEQ
N_SAMPLE = DEC_BATCH * DEC_SEQ
N_TOK = N_PROMPT + N_SAMPLE
SEQ_BLOCKS = N_TOK // DEC_SEQ
PROMPT_BLOCKS = N_PROMPT // DEC_SEQ
assert DEC_SEQ == 2 * SEQ and N_PROMPT % DEC_SEQ == 0

LANES = 128
F32_SUBLANES = 8
VMEM_LIMIT_BYTES = 56 * 1024 * 1024

TM = 512
NA_QROWS = 4
NA_BROWS = 12
NA_Q = NA_QROWS * GRID_W
NA_K = NA_BROWS * GRID_W
NA_UNROLL = 4
NEG = -1e30
DIFF_TQ = 512
DIFF_TK = 1024
DIFF_VT_ROWS = DIFF_V_DIM + 16
SAFE_SCORE = 50.0
NORM_SLACK = 1.02
FF_CHUNK = 256
HALO = F32_SUBLANES
Y_HALO = 16

_NT = (((1,), (1,)), ((), ()))


def _const_spec(shape):
    return pl.BlockSpec(shape, lambda *_: (0,) * len(shape), pipeline_mode=pl.Buffered(1))


def _layer_spec(layer, shape):
    return pl.BlockSpec((None,) + shape, lambda *_: (layer,) + (0,) * len(shape), pipeline_mode=pl.Buffered(1))


def _params(*semantics):
    return pltpu.CompilerParams(dimension_semantics=semantics, vmem_limit_bytes=VMEM_LIMIT_BYTES)


def _rms(x, g):
    return x * lax.rsqrt(jnp.mean(x * x, axis=-1, keepdims=True) + EPS) * g


def _split_heads(x):
    first = lax.broadcasted_iota(jnp.int32, (1, LANES), 1) < LANES // 2
    zero = jnp.zeros_like(x)
    return jnp.concatenate([jnp.where(first, x, zero), jnp.where(first, zero, x)], axis=0)


def _pos_block(i):
    tiles_prompt = N_PROMPT // TM
    return jnp.where(i < tiles_prompt, i % (SEQ // TM), (i - tiles_prompt) % (DEC_SEQ // TM))


def _inproj_kernel(*refs, first_layer):
    if first_layer:
        xp_ref, xs_ref, g_ref, w_ref, cos_ref, sin_lo_ref, sin_hi_ref = refs[:7]
        qa_ref, ka_ref, va_ref, qb_ref, kb_ref, vb_ref, xm_ref = refs[7:]
        x = jnp.where(pl.program_id(0) < N_PROMPT // TM, xp_ref[...], xs_ref[...])
        xm_ref[...] = x
    else:
        x_ref, g_ref, w_ref, cos_ref, sin_lo_ref, sin_hi_ref = refs[:6]
        qa_ref, ka_ref, va_ref, qb_ref, kb_ref, vb_ref = refs[6:]
        x = x_ref[...]
    n = _rms(x, g_ref[...]).astype(BF16)

    def proj(j):
        return jnp.dot(n, w_ref[:, j * SECTION:(j + 1) * SECTION], preferred_element_type=F32)

    qa_ref[...] = (proj(0) * (1.0 / math.sqrt(NA_HEAD_DIM))).astype(BF16)
    ka_ref[...] = proj(1).astype(BF16)
    va_ref[...] = proj(2).astype(BF16)
    cos, sin_lo, sin_hi = cos_ref[...], sin_lo_ref[...], sin_hi_ref[...]

    def rotary(x, scale):
        heads = []
        for h in range(DIFF_HEADS):
            xh = x[:, h * LANES:(h + 1) * LANES]
            r = xh * cos + pltpu.roll(xh, 96, 1) * sin_lo + pltpu.roll(xh, 32, 1) * sin_hi
            heads.append((r * scale).astype(BF16))
        return jnp.concatenate(heads, axis=1)

    qb_ref[...] = rotary(proj(3), 1.0 / math.sqrt(DIFF_QK_DIM))
    kb_ref[...] = rotary(proj(4), 1.0)
    vb_ref[...] = proj(5).astype(BF16)


def _inproj(xs, g, w, layer, cos, sin_lo, sin_hi):
    first_layer = len(xs) == 2
    tiles_prompt = N_PROMPT // TM
    tok = pl.BlockSpec((TM, D_MODEL), lambda i: (i, 0))
    pos = pl.BlockSpec((TM, LANES), lambda i: (_pos_block(i), 0))
    sec = pl.BlockSpec((TM, SECTION), lambda i: (i, 0))
    if first_layer:
        x_specs = [pl.BlockSpec((TM, D_MODEL), lambda i: (jnp.minimum(i, tiles_prompt - 1), 0)),
                   pl.BlockSpec((TM, D_MODEL), lambda i: (jnp.maximum(i - tiles_prompt, 0), 0))]
    else:
        x_specs = [tok]
    return pl.pallas_call(
        functools.partial(_inproj_kernel, first_layer=first_layer),
        grid=(N_TOK // TM,),
        in_specs=x_specs + [_const_spec((1, D_MODEL)), _layer_spec(layer, (D_MODEL, IN_WIDTH)), pos, pos, pos],
        out_specs=[sec] * 6 + [tok] * first_layer,
        out_shape=([jax.ShapeDtypeStruct((N_TOK, SECTION), BF16)] * 6
                   + [jax.ShapeDtypeStruct((N_TOK, D_MODEL), F32)] * first_layer),
        compiler_params=_params("parallel"),
        name="inproj_first" if first_layer else "inproj",
    )(*xs, g, w, cos, sin_lo, sin_hi)


def _rotary_tables():
    d = DIFF_QK_DIM
    inv_freq = 1.0 / (10000.0 ** (jnp.arange(0, d, 2, dtype=F32) / d))
    ang = jnp.arange(DEC_SEQ, dtype=F32)[:, None] * inv_freq[None, :]
    cos, sin, zero = jnp.cos(ang), jnp.sin(ang), jnp.zeros_like(ang)
    tile = lambda a, b: jnp.concatenate([a, b, a, b], axis=-1)
    return tile(cos, cos), tile(-sin, zero), tile(zero, sin)


def _na_row_patterns(rows):
    nblk = rows // NA_QROWS

    def one(j):
        r0 = j * NA_QROWS
        band = int(np.clip(r0 - WIN_H // 2, 0, rows - NA_BROWS))
        r = r0 + np.arange(NA_QROWS)
        start = np.clip(r - WIN_H // 2, 0, rows - WIN_H)
        kr = band + np.arange(NA_BROWS)
        ok = (kr[None, :] >= start[:, None]) & (kr[None, :] < start[:, None] + WIN_H)
        dr = kr[None, :] - r[:, None] + (WIN_H - 1)
        return ok, np.where(ok, dr, 0)

    pats = [one(0), one(1), one(nblk - 1)]
    for j in range(1, nblk - 1):
        assert all((a == b).all() for a, b in zip(one(j), pats[1]))
    return np.stack([p[0] for p in pats]), np.stack([p[1] for p in pats])


def _na_col_table(rpb_l):
    cols = np.arange(GRID_W)
    col_start = np.clip(cols - WIN_W // 2, 0, GRID_W - WIN_W)
    ok = (cols[None, :] >= col_start[:, None]) & (cols[None, :] < col_start[:, None] + WIN_W)
    dc = cols[None, :] - cols[:, None] + (WIN_W - 1)
    onehot = ((dc[None] == np.arange(2 * WIN_W - 1)[:, None, None]) & ok[None]).astype(np.float32)
    t = jnp.einsum("hrd,dck->hrck", rpb_l.astype(F32), jnp.asarray(onehot), precision=lax.Precision.HIGHEST)
    t = jnp.where(jnp.asarray(ok), t, NEG)
    return jnp.concatenate([t, t], axis=-1)


def _na_kernel(q_ref, k_ref, v_ref, tt_ref, o_ref, bias_ref, s_ref):
    blk = pl.program_id(1)

    @pl.when(blk == 0)
    def _():
        ok, dr = _na_row_patterns(SEQ // GRID_W)
        ok2, dr2 = _na_row_patterns(DEC_SEQ // GRID_W)
        assert (ok == ok2).all() and (dr == dr2).all()
        neg = jnp.full((GRID_W, GRID_W), NEG, F32)
        for pat in range(3):
            for hh in range(2):
                for qr in range(NA_QROWS):
                    r = hh * NA_Q + qr * GRID_W
                    for w in range(NA_BROWS):
                        half = (w % 2) * GRID_W
                        blockval = tt_ref[hh, int(dr[pat, qr, w]), :, half:half + GRID_W] if ok[pat, qr, w] else neg
                        bias_ref[pat, r:r + GRID_W, w * GRID_W:(w + 1) * GRID_W] = blockval

    is_prompt = blk < PROMPT_BLOCKS
    rows = jnp.where(is_prompt, SEQ // GRID_W, DEC_SEQ // GRID_W)
    nblk_p, nblk_s = SEQ // GRID_W // NA_QROWS, DEC_SEQ // GRID_W // NA_QROWS
    first = lax.broadcasted_iota(jnp.int32, (1, LANES), 1) < NA_HEAD_DIM

    def place(j):
        jr = jnp.where(is_prompt, j % nblk_p, j)
        base = jnp.where(is_prompt, (j // nblk_p) * (SEQ // GRID_W), 0)
        last = jnp.where(is_prompt, nblk_p, nblk_s) - 1
        r0 = jr * NA_QROWS
        band = jnp.clip(r0 - WIN_H // 2, 0, rows - NA_BROWS)
        pat = jnp.where(jr == 0, 0, jnp.where(jr == last, 2, 1))
        qoff = pl.multiple_of((base + r0) * GRID_W, NA_Q)
        koff = pl.multiple_of((base + band) * GRID_W, GRID_W)
        return qoff, koff, pat

    def scores(j, slot):
        qoff, koff, pat = place(j)
        qs = _split_heads(q_ref[pl.ds(qoff, NA_Q), :])
        s_ref[slot] = (lax.dot_general(qs, k_ref[pl.ds(koff, NA_K), :], _NT, preferred_element_type=F32)
                       + bias_ref[pat])

    def finish(j, slot, shifted):
        qoff, koff, _ = place(j)
        s = s_ref[slot]
        e = jnp.exp(s - jnp.max(s, axis=-1, keepdims=True) if shifted else s)
        l = jnp.sum(e, axis=-1, keepdims=True)
        o = jnp.dot(e.astype(BF16), v_ref[pl.ds(koff, NA_K), :], preferred_element_type=F32) / l
        o_ref[pl.ds(qoff, NA_Q), :] = jnp.where(first, o[:NA_Q], o[NA_Q:]).astype(BF16)

    lane_half = lambda axis: lax.broadcasted_iota(jnp.int32, (LANES, LANES), axis) // NA_HEAD_DIM
    same_half = jnp.where(lane_half(0) == lane_half(1), 1.0, 0.0).astype(BF16)

    def max_sq_norms(ref):
        x = ref[...]
        return jnp.max(jnp.dot(x * x, same_half, preferred_element_type=F32), axis=0, keepdims=True)

    tt = tt_ref[...]
    bias_max = jnp.max(jnp.where(tt > 0.5 * NEG, jnp.abs(tt), 0.0))
    qk_max_sq = jnp.max(max_sq_norms(q_ref) * max_sq_norms(k_ref)) * NORM_SLACK
    room = SAFE_SCORE - bias_max
    needs_max = jnp.logical_or(room <= 0.0, qk_max_sq > room * room)

    def run(shifted):
        def body(g, carry):
            for u in range(NA_UNROLL):
                j = g * NA_UNROLL + u
                scores(jnp.minimum(j + 1, nblk_s - 1), (u + 1) % 2)
                finish(j, u % 2, shifted)
            return carry

        def go():
            scores(0, 0)
            lax.fori_loop(0, nblk_s // NA_UNROLL, body, 0)
        return go

    lax.cond(needs_max, run(True), run(False))


def _na_attention(qa, ka, va, tt):
    blk = pl.BlockSpec((DEC_SEQ, LANES), lambda hp, b: (b, hp))
    return pl.pallas_call(
        _na_kernel,
        grid=(NA_HEADS // 2, SEQ_BLOCKS),
        in_specs=[blk, blk, blk,
                  pl.BlockSpec((2, 2 * WIN_H - 1, GRID_W, LANES), lambda hp, b: (hp, 0, 0, 0))],
        out_specs=blk,
        out_shape=jax.ShapeDtypeStruct((N_TOK, SECTION), BF16),
        scratch_shapes=[pltpu.VMEM((3, 2 * NA_Q, NA_K), F32), pltpu.VMEM((2, 2 * NA_Q, NA_K), F32)],
        compiler_params=_params("arbitrary", "arbitrary"),
        name="na_attention",
    )(qa, ka, va, tt)


def _diff_kernel(lq1_ref, lk1_ref, lq2_ref, lk2_ref, g_ref, q_ref, k_ref, v_ref, o_ref, s_ref, acc_ref, m_ref,
                 vt_ref, *, lam_init):
    is_prompt = pl.program_id(0) < PROMPT_BLOCKS
    n_qblocks = DEC_SEQ // DIFF_TQ

    def transpose_values(c, carry):
        rows = pl.ds(pl.multiple_of(c * DIFF_TQ, DIFF_TQ), DIFF_TQ)
        vt_ref[0:DIFF_V_DIM, rows] = v_ref[rows, :].astype(F32).T.astype(BF16)
        return carry

    lax.fori_loop(0, n_qblocks, transpose_values, 0)
    vt_ref[DIFF_V_DIM:, :] = jnp.ones((DIFF_VT_ROWS - DIFF_V_DIM, DEC_SEQ), BF16)

    def key_rows(qi, c, seq):
        base = (qi // (seq // DIFF_TQ)) * seq
        return pl.ds(pl.multiple_of(base + c * DIFF_TK, DIFF_TK), DIFF_TK)

    def scores(qi, c, slot, seq):
        q = q_ref[pl.ds(pl.multiple_of(qi * DIFF_TQ, DIFF_TQ), DIFF_TQ), :]
        s_ref[slot] = lax.dot_general(k_ref[key_rows(qi, c, seq), :], _split_heads(q), _NT,
                                      preferred_element_type=F32)

    def accumulate(qi, c, slot, seq, shifted):
        s = s_ref[slot]
        p = jnp.exp(s - m_ref[0:1, :] if shifted else s).astype(BF16)
        acc_ref[...] += jnp.dot(vt_ref[:, key_rows(qi, c, seq)], p, preferred_element_type=F32)

    def finish(qi):
        acc = acc_ref[...]
        sm1 = acc[:DIFF_V_DIM, :DIFF_TQ] / acc[DIFF_V_DIM:DIFF_V_DIM + 1, :DIFF_TQ]
        sm2 = acc[:DIFF_V_DIM, DIFF_TQ:] / acc[DIFF_V_DIM:DIFF_V_DIM + 1, DIFF_TQ:]
        lam = (jnp.exp(jnp.sum(lq1_ref[...] * lk1_ref[...], axis=-1, keepdims=True))
               - jnp.exp(jnp.sum(lq2_ref[...] * lk2_ref[...], axis=-1, keepdims=True)) + lam_init)
        d = sm1 - lam * sm2
        d = d * lax.rsqrt(jnp.mean(d * d, axis=0, keepdims=True) + EPS)
        y = d.T * g_ref[...] * (1.0 - lam_init)
        o_ref[pl.ds(pl.multiple_of(qi * DIFF_TQ, DIFF_TQ), DIFF_TQ), :] = y.astype(BF16)
        acc_ref[...] = jnp.zeros_like(acc_ref)

    lane_half = lambda axis: lax.broadcasted_iota(jnp.int32, (LANES, LANES), axis) // DIFF_QK_DIM
    same_half = jnp.where(lane_half(0) == lane_half(1), 1.0, 0.0).astype(BF16)

    def max_sq_norms(ref):
        x = ref[...]
        return jnp.max(jnp.dot(x * x, same_half, preferred_element_type=F32), axis=0, keepdims=True)

    bound_sq = jnp.max(max_sq_norms(q_ref) * max_sq_norms(k_ref)) * NORM_SLACK
    needs_max = bound_sq > SAFE_SCORE ** 2
    acc_ref[...] = jnp.zeros_like(acc_ref)

    def unshifted(seq):
        nchunks = seq // DIFF_TK
        assert nchunks % 2 == 0

        def body(qi, carry):
            for c in range(nchunks):
                if c + 1 < nchunks:
                    scores(qi, c + 1, (c + 1) % 2, seq)
                else:
                    scores(jnp.minimum(qi + 1, n_qblocks - 1), 0, 0, seq)
                accumulate(qi, c, c % 2, seq, False)
            finish(qi)
            return carry

        def run():
            scores(0, 0, 0, seq)
            lax.fori_loop(0, n_qblocks, body, 0)
        return run

    def shifted(seq):
        nchunks = seq // DIFF_TK

        def qblock(qi, carry):
            def col_max(c, m):
                scores(qi, c, 0, seq)
                return jnp.maximum(m, jnp.max(s_ref[0], axis=0, keepdims=True))

            m = lax.fori_loop(0, nchunks, col_max, jnp.full((1, 2 * DIFF_TQ), NEG, F32))
            m_ref[...] = jnp.broadcast_to(m, m_ref.shape)

            def chunk(c, carry):
                scores(qi, c, 0, seq)
                accumulate(qi, c, 0, seq, True)
                return carry

            lax.fori_loop(0, nchunks, chunk, 0)
            finish(qi)
            return carry

        def run():
            lax.fori_loop(0, n_qblocks, qblock, 0)
        return run

    lax.cond(needs_max,
             lambda: lax.cond(is_prompt, shifted(SEQ), shifted(DEC_SEQ)),
             lambda: lax.cond(is_prompt, unshifted(SEQ), unshifted(DEC_SEQ)))


def _diff_attention(lams, g, qb, kb, vb, *, lam_init):
    vec = _const_spec((1, DIFF_QK_DIM))
    blk = pl.BlockSpec((DEC_SEQ, LANES), lambda b, h: (b, h))
    return pl.pallas_call(
        functools.partial(_diff_kernel, lam_init=lam_init),
        grid=(SEQ_BLOCKS, DIFF_HEADS),
        in_specs=[vec, vec, vec, vec, _const_spec((1, DIFF_V_DIM)), blk, blk, blk],
        out_specs=blk,
        out_shape=jax.ShapeDtypeStruct((N_TOK, SECTION), BF16),
        scratch_shapes=[pltpu.VMEM((2, DIFF_TK, 2 * DIFF_TQ), F32), pltpu.VMEM((DIFF_VT_ROWS, 2 * DIFF_TQ), F32),
                        pltpu.VMEM((F32_SUBLANES, 2 * DIFF_TQ), F32), pltpu.VMEM((DIFF_VT_ROWS, DEC_SEQ), BF16)],
        compiler_params=_params("parallel", "parallel"),
        name="diff_attention",
    )(*lams, g, qb, kb, vb)


def _ffn_kernel(x_ref, xp_ref, xn_ref, ya_ref, yap_ref, yan_ref, yb_ref, ybp_ref, ybn_ref, wo_ref, g_ref, wup_ref,
                cw_ref, cb_ref, wdn_ref, gf_ref, *out_and_scratch, final):
    h_ref = out_and_scratch[-1]
    i = pl.program_id(0)
    tiles_prompt = N_PROMPT // TM
    per_seq = jnp.where(i < tiles_prompt, SEQ // TM, DEC_SEQ // TM)
    has_prev = (i % per_seq) != 0
    has_next = (i % per_seq) != per_seq - 1

    ext = TM + 2 * HALO
    ya = jnp.concatenate([yap_ref[...], ya_ref[...], yan_ref[...]], axis=0)
    yb = jnp.concatenate([ybp_ref[...], yb_ref[...], ybn_ref[...]], axis=0)
    attn = (jnp.dot(ya, wo_ref[:SECTION, :], preferred_element_type=F32)
            + jnp.dot(yb, wo_ref[SECTION:, :], preferred_element_type=F32))
    xe = (jnp.concatenate([xp_ref[...], x_ref[...], xn_ref[...]], axis=0)
          + attn[Y_HALO - HALO:Y_HALO - HALO + ext])
    x = xe[HALO:HALO + TM]
    n = _rms(xe, g_ref[...]).astype(BF16)
    row = lax.broadcasted_iota(jnp.int32, (ext, 1), 0)
    keep = ((row >= HALO) | has_prev) & ((row < HALO + TM) | has_next)

    for c in range(D_FF // FF_CHUNK):
        lo = c * FF_CHUNK
        gate = jnp.dot(n, wup_ref[:, lo:lo + FF_CHUNK], preferred_element_type=F32)
        val = jnp.dot(n, wup_ref[:, D_FF + lo:D_FF + lo + FF_CHUNK], preferred_element_type=F32)
        gate = jnp.where(keep, gate, 0.0)
        cw = cw_ref[:, lo:lo + FF_CHUNK]
        conv = (pltpu.roll(gate, 1, 0) * cw[0:1] + gate * cw[1:2] + pltpu.roll(gate, ext - 1, 0) * cw[2:3]
                + cb_ref[:, lo:lo + FF_CHUNK])
        act = 0.5 * conv * (1.0 + lax.erf(conv * (1.0 / math.sqrt(2.0))))
        h_ref[:, lo:lo + FF_CHUNK] = (act * val)[HALO:HALO + TM].astype(BF16)

    y = x + jnp.dot(h_ref[...], wdn_ref[...], preferred_element_type=F32)
    if final:
        prompt_ref, sample_ref = out_and_scratch[:2]
        res = _rms(y, gf_ref[...])

        @pl.when(i < tiles_prompt)
        def _():
            prompt_ref[...] = res

        @pl.when(i >= tiles_prompt)
        def _():
            sample_ref[...] = res
    else:
        out_and_scratch[0][...] = y


def _ffn(x, ya, yb, wo, g, wup, cw, cb, wdn, gf, *, layer, final):
    def tile_and_halos(width, halo):
        per_tile = TM // halo
        return [pl.BlockSpec((TM, width), lambda i: (i, 0)),
                pl.BlockSpec((halo, width), lambda i: (jnp.maximum(i * per_tile - 1, 0), 0)),
                pl.BlockSpec((halo, width), lambda i: (jnp.minimum((i + 1) * per_tile, N_TOK // halo - 1), 0))]

    x_specs = tile_and_halos(D_MODEL, HALO)
    y_specs = tile_and_halos(SECTION, Y_HALO)
    tiles_prompt = N_PROMPT // TM
    if final:
        out_specs = [pl.BlockSpec((TM, D_MODEL), lambda i: (jnp.minimum(i, tiles_prompt - 1), 0)),
                     pl.BlockSpec((TM, D_MODEL), lambda i: (jnp.maximum(i - tiles_prompt, 0), 0))]
        out_shape = [jax.ShapeDtypeStruct((N_PROMPT, D_MODEL), F32), jax.ShapeDtypeStruct((N_SAMPLE, D_MODEL), F32)]
    else:
        out_specs, out_shape = x_specs[0], jax.ShapeDtypeStruct((N_TOK, D_MODEL), F32)
    return pl.pallas_call(
        functools.partial(_ffn_kernel, final=final),
        grid=(N_TOK // TM,),
        in_specs=x_specs + y_specs + y_specs + [
            _layer_spec(layer, (2 * SECTION, D_MODEL)), _const_spec((1, D_MODEL)),
            _layer_spec(layer, (D_MODEL, 2 * D_FF)), _const_spec((3, D_FF)), _const_spec((1, D_FF)),
            _layer_spec(layer, (D_FF, D_MODEL)), _const_spec((1, D_MODEL))],
        out_specs=out_specs,
        out_shape=out_shape,
        scratch_shapes=[pltpu.VMEM((TM, D_FF), BF16)],
        compiler_params=_params("arbitrary"),
        name="ffn_final" if final else "ffn",
    )(x, x, x, ya, ya, ya, yb, yb, yb, wo, g, wup, cw, cb, wdn, gf)


def kernel(x_prompt, x_sample, g_attn, w_in, rpb, lam_q1, lam_k1, lam_q2, lam_k2, subln_g, w_out, g_ffn, w_up,
           conv_w, conv_b, w_down, g_final):
    cos, sin_lo, sin_hi = _rotary_tables()
    row = lambda v: v.reshape(1, -1).astype(F32)
    xs = (x_prompt.reshape(N_PROMPT, D_MODEL), x_sample.reshape(N_SAMPLE, D_MODEL))
    w_in, w_out, w_up, w_down = (w.astype(BF16) for w in (w_in, w_out, w_up, w_down))
    for l in range(DEPTH):
        lam_init = 0.8 - 0.6 * math.exp(-0.3 * l)
        qa, ka, va, qb, kb, vb, *merged = _inproj(xs, row(g_attn[l]), w_in, l, cos, sin_lo, sin_hi)
        x = merged[0] if merged else xs[0]
        ya = _na_attention(qa, ka, va, _na_col_table(rpb[l]))
        lams = (row(lam_q1[l]), row(lam_k1[l]), row(lam_q2[l]), row(lam_k2[l]))
        yb = _diff_attention(lams, row(subln_g[l]), qb, kb, vb, lam_init=lam_init)
        x = _ffn(x, ya, yb, w_out, row(g_ffn[l]), w_up, conv_w[l].astype(F32), row(conv_b[l]), w_down, row(g_final),
                 layer=l, final=(l == DEPTH - 1))
        xs = (x,)
    y_prompt, y_sample = x
    return (y_prompt.reshape(BATCH, SEQ, D_MODEL), y_sample.reshape(DEC_BATCH, DEC_SEQ, D_MODEL))
```

```python
import functools
import math

import jax
import jax.numpy as jnp
import numpy as np
from jax import lax
from jax.experimental import pallas as pl
from jax.experimental.pallas import tpu as pltpu

F32 = jnp.float32
BF16 = jnp.bfloat16

D_MODEL = 1024
BATCH, SEQ = 4, 4096
DEC_BATCH, DEC_SEQ = 2, 8192
DEPTH = 4
GRID_W = 64
WIN_H, WIN_W = 8, 16
NA_HEADS, NA_HEAD_DIM = 8, 64
DIFF_HEADS, DIFF_QK_DIM, DIFF_V_DIM = 4, 64, 128
SECTION = 512
IN_WIDTH = 6 * SECTION
D_FF = 2816
EPS = 1e-6

N_PROMPT = BATCH * SEQ
N_SAMPLE = DEC_BATCH * DEC_SEQ
N_TOK = N_PROMPT + N_SAMPLE
SEQ_BLOCKS = N_TOK // DEC_SEQ
PROMPT_BLOCKS = N_PROMPT // DEC_SEQ
assert DEC_SEQ == 2 * SEQ and N_PROMPT % DEC_SEQ == 0

LANES = 128
F32_SUBLANES = 8
VMEM_LIMIT_BYTES = 56 * 1024 * 1024

TM = 512
NA_QROWS = 4
NA_BROWS = 12
NA_Q = NA_QROWS * GRID_W
NA_K = NA_BROWS * GRID_W
NA_UNROLL = 4
NEG = -1e30
DIFF_TQ = 512
DIFF_TK = 1024
DIFF_VT_ROWS = DIFF_V_DIM + 16
SAFE_SCORE = 50.0
NORM_SLACK = 1.02
FF_CHUNK = 256
HALO = F32_SUBLANES
Y_HALO = 16

_NT = (((1,), (1,)), ((), ()))


def _const_spec(shape):
    return pl.BlockSpec(shape, lambda *_: (0,) * len(shape), pipeline_mode=pl.Buffered(1))


def _layer_spec(layer, shape):
    return pl.BlockSpec((None,) + shape, lambda *_: (layer,) + (0,) * len(shape), pipeline_mode=pl.Buffered(1))


def _params(*semantics):
    return pltpu.CompilerParams(dimension_semantics=semantics, vmem_limit_bytes=VMEM_LIMIT_BYTES)


def _rms(x, g):
    return x * lax.rsqrt(jnp.mean(x * x, axis=-1, keepdims=True) + EPS) * g


def _split_heads(x):
    first = lax.broadcasted_iota(jnp.int32, (1, LANES), 1) < LANES // 2
    zero = jnp.zeros_like(x)
    return jnp.concatenate([jnp.where(first, x, zero), jnp.where(first, zero, x)], axis=0)


def _pos_block(i):
    tiles_prompt = N_PROMPT // TM
    return jnp.where(i < tiles_prompt, i % (SEQ // TM), (i - tiles_prompt) % (DEC_SEQ // TM))


def _inproj_kernel(*refs, first_layer):
    if first_layer:
        xp_ref, xs_ref, g_ref, w_ref, cos_ref, sin_lo_ref, sin_hi_ref = refs[:7]
        qa_ref, ka_ref, va_ref, qb_ref, kb_ref, vb_ref, xm_ref = refs[7:]
        x = jnp.where(pl.program_id(0) < N_PROMPT // TM, xp_ref[...], xs_ref[...])
        xm_ref[...] = x
    else:
        x_ref, g_ref, w_ref, cos_ref, sin_lo_ref, sin_hi_ref = refs[:6]
        qa_ref, ka_ref, va_ref, qb_ref, kb_ref, vb_ref = refs[6:]
        x = x_ref[...]
    n = _rms(x, g_ref[...]).astype(BF16)

    def proj(j):
        return jnp.dot(n, w_ref[:, j * SECTION:(j + 1) * SECTION], preferred_element_type=F32)

    qa_ref[...] = (proj(0) * (1.0 / math.sqrt(NA_HEAD_DIM))).astype(BF16)
    ka_ref[...] = proj(1).astype(BF16)
    va_ref[...] = proj(2).astype(BF16)
    cos, sin_lo, sin_hi = cos_ref[...], sin_lo_ref[...], sin_hi_ref[...]

    def rotary(x, scale):
        heads = []
        for h in range(DIFF_HEADS):
            xh = x[:, h * LANES:(h + 1) * LANES]
            r = xh * cos + pltpu.roll(xh, 96, 1) * sin_lo + pltpu.roll(xh, 32, 1) * sin_hi
            heads.append((r * scale).astype(BF16))
        return jnp.concatenate(heads, axis=1)

    qb_ref[...] = rotary(proj(3), 1.0 / math.sqrt(DIFF_QK_DIM))
    kb_ref[...] = rotary(proj(4), 1.0)
    vb_ref[...] = proj(5).astype(BF16)


def _inproj(xs, g, w, layer, cos, sin_lo, sin_hi):
    first_layer = len(xs) == 2
    tiles_prompt = N_PROMPT // TM
    tok = pl.BlockSpec((TM, D_MODEL), lambda i: (i, 0))
    pos = pl.BlockSpec((TM, LANES), lambda i: (_pos_block(i), 0))
    sec = pl.BlockSpec((TM, SECTION), lambda i: (i, 0))
    if first_layer:
        x_specs = [pl.BlockSpec((TM, D_MODEL), lambda i: (jnp.minimum(i, tiles_prompt - 1), 0)),
                   pl.BlockSpec((TM, D_MODEL), lambda i: (jnp.maximum(i - tiles_prompt, 0), 0))]
    else:
        x_specs = [tok]
    return pl.pallas_call(
        functools.partial(_inproj_kernel, first_layer=first_layer),
        grid=(N_TOK // TM,),
        in_specs=x_specs + [_const_spec((1, D_MODEL)), _layer_spec(layer, (D_MODEL, IN_WIDTH)), pos, pos, pos],
        out_specs=[sec] * 6 + [tok] * first_layer,
        out_shape=([jax.ShapeDtypeStruct((N_TOK, SECTION), BF16)] * 6
                   + [jax.ShapeDtypeStruct((N_TOK, D_MODEL), F32)] * first_layer),
        compiler_params=_params("parallel"),
        name="inproj_first" if first_layer else "inproj",
    )(*xs, g, w, cos, sin_lo, sin_hi)


def _rotary_tables():
    d = DIFF_QK_DIM
    inv_freq = 1.0 / (10000.0 ** (jnp.arange(0, d, 2, dtype=F32) / d))
    ang = jnp.arange(DEC_SEQ, dtype=F32)[:, None] * inv_freq[None, :]
    cos, sin, zero = jnp.cos(ang), jnp.sin(ang), jnp.zeros_like(ang)
    tile = lambda a, b: jnp.concatenate([a, b, a, b], axis=-1)
    return tile(cos, cos), tile(-sin, zero), tile(zero, sin)


def _na_row_patterns(rows):
    nblk = rows // NA_QROWS

    def one(j):
        r0 = j * NA_QROWS
        band = int(np.clip(r0 - WIN_H // 2, 0, rows - NA_BROWS))
        r = r0 + np.arange(NA_QROWS)
        start = np.clip(r - WIN_H // 2, 0, rows - WIN_H)
        kr = band + np.arange(NA_BROWS)
        ok = (kr[None, :] >= start[:, None]) & (kr[None, :] < start[:, None] + WIN_H)
        dr = kr[None, :] - r[:, None] + (WIN_H - 1)
        return ok, np.where(ok, dr, 0)

    pats = [one(0), one(1), one(nblk - 1)]
    for j in range(1, nblk - 1):
        assert all((a == b).all() for a, b in zip(one(j), pats[1]))
    return np.stack([p[0] for p in pats]), np.stack([p[1] for p in pats])


def _na_col_table(rpb_l):
    cols = np.arange(GRID_W)
    col_start = np.clip(cols - WIN_W // 2, 0, GRID_W - WIN_W)
    ok = (cols[None, :] >= col_start[:, None]) & (cols[None, :] < col_start[:, None] + WIN_W)
    dc = cols[None, :] - cols[:, None] + (WIN_W - 1)
    onehot = ((dc[None] == np.arange(2 * WIN_W - 1)[:, None, None]) & ok[None]).astype(np.float32)
    t = jnp.einsum("hrd,dck->hrck", rpb_l.astype(F32), jnp.asarray(onehot), precision=lax.Precision.HIGHEST)
    t = jnp.where(jnp.asarray(ok), t, NEG)
    return jnp.concatenate([t, t], axis=-1)


def _na_kernel(q_ref, k_ref, v_ref, tt_ref, o_ref, bias_ref, s_ref):
    blk = pl.program_id(1)

    @pl.when(blk == 0)
    def _():
        ok, dr = _na_row_patterns(SEQ // GRID_W)
        ok2, dr2 = _na_row_patterns(DEC_SEQ // GRID_W)
        assert (ok == ok2).all() and (dr == dr2).all()
        neg = jnp.full((GRID_W, GRID_W), NEG, F32)
        for pat in range(3):
            for hh in range(2):
                for qr in range(NA_QROWS):
                    r = hh * NA_Q + qr * GRID_W
                    for w in range(NA_BROWS):
                        half = (w % 2) * GRID_W
                        blockval = tt_ref[hh, int(dr[pat, qr, w]), :, half:half + GRID_W] if ok[pat, qr, w] else neg
                        bias_ref[pat, r:r + GRID_W, w * GRID_W:(w + 1) * GRID_W] = blockval

    is_prompt = blk < PROMPT_BLOCKS
    rows = jnp.where(is_prompt, SEQ // GRID_W, DEC_SEQ // GRID_W)
    nblk_p, nblk_s = SEQ // GRID_W // NA_QROWS, DEC_SEQ // GRID_W // NA_QROWS
    first = lax.broadcasted_iota(jnp.int32, (1, LANES), 1) < NA_HEAD_DIM
    ones = jnp.ones((NA_K, LANES), BF16)

    def place(j):
        jr = jnp.where(is_prompt, j % nblk_p, j)
        base = jnp.where(is_prompt, (j // nblk_p) * (SEQ // GRID_W), 0)
        last = jnp.where(is_prompt, nblk_p, nblk_s) - 1
        r0 = jr * NA_QROWS
        band = jnp.clip(r0 - WIN_H // 2, 0, rows - NA_BROWS)
        pat = jnp.where(jr == 0, 0, jnp.where(jr == last, 2, 1))
        qoff = pl.multiple_of((base + r0) * GRID_W, NA_Q)
        koff = pl.multiple_of((base + band) * GRID_W, GRID_W)
        return qoff, koff, pat

    def scores(j, slot):
        qoff, koff, pat = place(j)
        qs = _split_heads(q_ref[pl.ds(qoff, NA_Q), :])
        s_ref[slot] = (lax.dot_general(qs, k_ref[pl.ds(koff, NA_K), :], _NT, preferred_element_type=F32)
                       + bias_ref[pat])

    def finish(j, slot, shifted):
        qoff, koff, _ = place(j)
        s = s_ref[slot]
        e = jnp.exp(s - jnp.max(s, axis=-1, keepdims=True) if shifted else s).astype(BF16)
        o = jnp.dot(e, jnp.concatenate([v_ref[pl.ds(koff, NA_K), :], ones], axis=1), preferred_element_type=F32)
        o = o[:, :LANES] / o[:, LANES:]
        o_ref[pl.ds(qoff, NA_Q), :] = jnp.where(first, o[:NA_Q], o[NA_Q:]).astype(BF16)

    lane_half = lambda axis: lax.broadcasted_iota(jnp.int32, (LANES, LANES), axis) // NA_HEAD_DIM
    same_half = jnp.where(lane_half(0) == lane_half(1), 1.0, 0.0).astype(BF16)

    def max_sq_norms(ref):
        x = ref[...]
        return jnp.max(jnp.dot(x * x, same_half, preferred_element_type=F32), axis=0, keepdims=True)

    tt = tt_ref[...]
    bias_max = jnp.max(jnp.where(tt > 0.5 * NEG, jnp.abs(tt), 0.0))
    qk_max_sq = jnp.max(max_sq_norms(q_ref) * max_sq_norms(k_ref)) * NORM_SLACK
    room = SAFE_SCORE - bias_max
    needs_max = jnp.logical_or(room <= 0.0, qk_max_sq > room * room)

    def run(shifted):
        def body(g, carry):
            for u in range(NA_UNROLL):
                j = g * NA_UNROLL + u
                scores(jnp.minimum(j + 1, nblk_s - 1), (u + 1) % 2)
                finish(j, u % 2, shifted)
            return carry

        def go():
            scores(0, 0)
            lax.fori_loop(0, nblk_s // NA_UNROLL, body, 0)
        return go

    lax.cond(needs_max, run(True), run(False))


def _na_attention(qa, ka, va, tt):
    blk = pl.BlockSpec((DEC_SEQ, LANES), lambda hp, b: (b, hp))
    return pl.pallas_call(
        _na_kernel,
        grid=(NA_HEADS // 2, SEQ_BLOCKS),
        in_specs=[blk, blk, blk,
                  pl.BlockSpec((2, 2 * WIN_H - 1, GRID_W, LANES), lambda hp, b: (hp, 0, 0, 0))],
        out_specs=blk,
        out_shape=jax.ShapeDtypeStruct((N_TOK, SECTION), BF16),
        scratch_shapes=[pltpu.VMEM((3, 2 * NA_Q, NA_K), F32), pltpu.VMEM((2, 2 * NA_Q, NA_K), F32)],
        compiler_params=_params("arbitrary", "arbitrary"),
        name="na_attention",
    )(qa, ka, va, tt)


def _diff_kernel(lq1_ref, lk1_ref, lq2_ref, lk2_ref, g_ref, q_ref, k_ref, v_ref, o_ref, s_ref, acc_ref, m_ref,
                 vt_ref, *, lam_init):
    is_prompt = pl.program_id(0) < PROMPT_BLOCKS
    n_qblocks = DEC_SEQ // DIFF_TQ

    def transpose_values(c, carry):
        rows = pl.ds(pl.multiple_of(c * DIFF_TQ, DIFF_TQ), DIFF_TQ)
        vt_ref[0:DIFF_V_DIM, rows] = v_ref[rows, :].astype(F32).T.astype(BF16)
        return carry

    lax.fori_loop(0, n_qblocks, transpose_values, 0)
    vt_ref[DIFF_V_DIM:, :] = jnp.ones((DIFF_VT_ROWS - DIFF_V_DIM, DEC_SEQ), BF16)

    def key_rows(qi, c, seq):
        base = (qi // (seq // DIFF_TQ)) * seq
        return pl.ds(pl.multiple_of(base + c * DIFF_TK, DIFF_TK), DIFF_TK)

    def scores(qi, c, slot, seq):
        q = q_ref[pl.ds(pl.multiple_of(qi * DIFF_TQ, DIFF_TQ), DIFF_TQ), :]
        s_ref[slot] = lax.dot_general(k_ref[key_rows(qi, c, seq), :], _split_heads(q), _NT,
                                      preferred_element_type=F32)

    def accumulate(qi, c, slot, seq, shifted):
        s = s_ref[slot]
        p = jnp.exp(s - m_ref[0:1, :] if shifted else s).astype(BF16)
        acc_ref[...] += jnp.dot(vt_ref[:, key_rows(qi, c, seq)], p, preferred_element_type=F32)

    def finish(qi):
        acc = acc_ref[...]
        sm1 = acc[:DIFF_V_DIM, :DIFF_TQ] / acc[DIFF_V_DIM:DIFF_V_DIM + 1, :DIFF_TQ]
        sm2 = acc[:DIFF_V_DIM, DIFF_TQ:] / acc[DIFF_V_DIM:DIFF_V_DIM + 1, DIFF_TQ:]
        lam = (jnp.exp(jnp.sum(lq1_ref[...] * lk1_ref[...], axis=-1, keepdims=True))
               - jnp.exp(jnp.sum(lq2_ref[...] * lk2_ref[...], axis=-1, keepdims=True)) + lam_init)
        d = sm1 - lam * sm2
        d = d * lax.rsqrt(jnp.mean(d * d, axis=0, keepdims=True) + EPS)
        y = d.T * g_ref[...] * (1.0 - lam_init)
        o_ref[pl.ds(pl.multiple_of(qi * DIFF_TQ, DIFF_TQ), DIFF_TQ), :] = y.astype(BF16)
        acc_ref[...] = jnp.zeros_like(acc_ref)

    lane_half = lambda axis: lax.broadcasted_iota(jnp.int32, (LANES, LANES), axis) // DIFF_QK_DIM
    same_half = jnp.where(lane_half(0) == lane_half(1), 1.0, 0.0).astype(BF16)

    def max_sq_norms(ref):
        x = ref[...]
        return jnp.max(jnp.dot(x * x, same_half, preferred_element_type=F32), axis=0, keepdims=True)

    bound_sq = jnp.max(max_sq_norms(q_ref) * max_sq_norms(k_ref)) * NORM_SLACK
    needs_max = bound_sq > SAFE_SCORE ** 2
    acc_ref[...] = jnp.zeros_like(acc_ref)

    def unshifted(seq):
        nchunks = seq // DIFF_TK
        assert nchunks % 2 == 0

        def body(qi, carry):
            for c in range(nchunks):
                if c + 1 < nchunks:
                    scores(qi, c + 1, (c + 1) % 2, seq)
                else:
                    scores(jnp.minimum(qi + 1, n_qblocks - 1), 0, 0, seq)
                accumulate(qi, c, c % 2, seq, False)
            finish(qi)
            return carry

        def run():
            scores(0, 0, 0, seq)
            lax.fori_loop(0, n_qblocks, body, 0)
        return run

    def shifted(seq):
        nchunks = seq // DIFF_TK

        def qblock(qi, carry):
            def col_max(c, m):
                scores(qi, c, 0, seq)
                return jnp.maximum(m, jnp.max(s_ref[0], axis=0, keepdims=True))

            m = lax.fori_loop(0, nchunks, col_max, jnp.full((1, 2 * DIFF_TQ), NEG, F32))
            m_ref[...] = jnp.broadcast_to(m, m_ref.shape)

            def chunk(c, carry):
                scores(qi, c, 0, seq)
                accumulate(qi, c, 0, seq, True)
                return carry

            lax.fori_loop(0, nchunks, chunk, 0)
            finish(qi)
            return carry

        def run():
            lax.fori_loop(0, n_qblocks, qblock, 0)
        return run

    lax.cond(needs_max,
             lambda: lax.cond(is_prompt, shifted(SEQ), shifted(DEC_SEQ)),
             lambda: lax.cond(is_prompt, unshifted(SEQ), unshifted(DEC_SEQ)))


def _diff_attention(lams, g, qb, kb, vb, *, lam_init):
    vec = _const_spec((1, DIFF_QK_DIM))
    blk = pl.BlockSpec((DEC_SEQ, LANES), lambda b, h: (b, h))
    return pl.pallas_call(
        functools.partial(_diff_kernel, lam_init=lam_init),
        grid=(SEQ_BLOCKS, DIFF_HEADS),
        in_specs=[vec, vec, vec, vec, _const_spec((1, DIFF_V_DIM)), blk, blk, blk],
        out_specs=blk,
        out_shape=jax.ShapeDtypeStruct((N_TOK, SECTION), BF16),
        scratch_shapes=[pltpu.VMEM((2, DIFF_TK, 2 * DIFF_TQ), F32), pltpu.VMEM((DIFF_VT_ROWS, 2 * DIFF_TQ), F32),
                        pltpu.VMEM((F32_SUBLANES, 2 * DIFF_TQ), F32), pltpu.VMEM((DIFF_VT_ROWS, DEC_SEQ), BF16)],
        compiler_params=_params("parallel", "parallel"),
        name="diff_attention",
    )(*lams, g, qb, kb, vb)


def _ffn_kernel(x_ref, xp_ref, xn_ref, ya_ref, yap_ref, yan_ref, yb_ref, ybp_ref, ybn_ref, wo_ref, g_ref, wup_ref,
                cw_ref, cb_ref, wdn_ref, gf_ref, *out_and_scratch, final):
    h_ref = out_and_scratch[-1]
    i = pl.program_id(0)
    tiles_prompt = N_PROMPT // TM
    per_seq = jnp.where(i < tiles_prompt, SEQ // TM, DEC_SEQ // TM)
    has_prev = (i % per_seq) != 0
    has_next = (i % per_seq) != per_seq - 1

    ext = TM + 2 * HALO
    ya = jnp.concatenate([yap_ref[...], ya_ref[...], yan_ref[...]], axis=0)
    yb = jnp.concatenate([ybp_ref[...], yb_ref[...], ybn_ref[...]], axis=0)
    attn = (jnp.dot(ya, wo_ref[:SECTION, :], preferred_element_type=F32)
            + jnp.dot(yb, wo_ref[SECTION:, :], preferred_element_type=F32))
    xe = (jnp.concatenate([xp_ref[...], x_ref[...], xn_ref[...]], axis=0)
          + attn[Y_HALO - HALO:Y_HALO - HALO + ext])
    x = xe[HALO:HALO + TM]
    n = _rms(xe, g_ref[...]).astype(BF16)
    row = lax.broadcasted_iota(jnp.int32, (ext, 1), 0)
    keep = ((row >= HALO) | has_prev) & ((row < HALO + TM) | has_next)

    for c in range(D_FF // FF_CHUNK):
        lo = c * FF_CHUNK
        gate = jnp.dot(n, wup_ref[:, lo:lo + FF_CHUNK], preferred_element_type=F32)
        val = jnp.dot(n, wup_ref[:, D_FF + lo:D_FF + lo + FF_CHUNK], preferred_element_type=F32)
        gate = jnp.where(keep, gate, 0.0)
        cw = cw_ref[:, lo:lo + FF_CHUNK]
        conv = (pltpu.roll(gate, 1, 0) * cw[0:1] + gate * cw[1:2] + pltpu.roll(gate, ext - 1, 0) * cw[2:3]
                + cb_ref[:, lo:lo + FF_CHUNK])
        act = 0.5 * conv * (1.0 + lax.erf(conv * (1.0 / math.sqrt(2.0))))
        h_ref[:, lo:lo + FF_CHUNK] = (act * val)[HALO:HALO + TM].astype(BF16)

    y = x + jnp.dot(h_ref[...], wdn_ref[...], preferred_element_type=F32)
    if final:
        prompt_ref, sample_ref = out_and_scratch[:2]
        res = _rms(y, gf_ref[...])

        @pl.when(i < tiles_prompt)
        def _():
            prompt_ref[...] = res

        @pl.when(i >= tiles_prompt)
        def _():
            sample_ref[...] = res
    else:
        out_and_scratch[0][...] = y


def _ffn(x, ya, yb, wo, g, wup, cw, cb, wdn, gf, *, layer, final):
    def tile_and_halos(width, halo):
        per_tile = TM // halo
        return [pl.BlockSpec((TM, width), lambda i: (i, 0)),
                pl.BlockSpec((halo, width), lambda i: (jnp.maximum(i * per_tile - 1, 0), 0)),
                pl.BlockSpec((halo, width), lambda i: (jnp.minimum((i + 1) * per_tile, N_TOK // halo - 1), 0))]

    x_specs = tile_and_halos(D_MODEL, HALO)
    y_specs = tile_and_halos(SECTION, Y_HALO)
    tiles_prompt = N_PROMPT // TM
    if final:
        out_specs = [pl.BlockSpec((TM, D_MODEL), lambda i: (jnp.minimum(i, tiles_prompt - 1), 0)),
                     pl.BlockSpec((TM, D_MODEL), lambda i: (jnp.maximum(i - tiles_prompt, 0), 0))]
        out_shape = [jax.ShapeDtypeStruct((N_PROMPT, D_MODEL), F32), jax.ShapeDtypeStruct((N_SAMPLE, D_MODEL), F32)]
    else:
        out_specs, out_shape = x_specs[0], jax.ShapeDtypeStruct((N_TOK, D_MODEL), F32)
    return pl.pallas_call(
        functools.partial(_ffn_kernel, final=final),
        grid=(N_TOK // TM,),
        in_specs=x_specs + y_specs + y_specs + [
            _layer_spec(layer, (2 * SECTION, D_MODEL)), _const_spec((1, D_MODEL)),
            _layer_spec(layer, (D_MODEL, 2 * D_FF)), _const_spec((3, D_FF)), _const_spec((1, D_FF)),
            _layer_spec(layer, (D_FF, D_MODEL)), _const_spec((1, D_MODEL))],
        out_specs=out_specs,
        out_shape=out_shape,
        scratch_shapes=[pltpu.VMEM((TM, D_FF), BF16)],
        compiler_params=_params("arbitrary"),
        name="ffn_final" if final else "ffn",
    )(x, x, x, ya, ya, ya, yb, yb, yb, wo, g, wup, cw, cb, wdn, gf)


def kernel(x_prompt, x_sample, g_attn, w_in, rpb, lam_q1, lam_k1, lam_q2, lam_k2, subln_g, w_out, g_ffn, w_up,
           conv_w, conv_b, w_down, g_final):
    cos, sin_lo, sin_hi = _rotary_tables()
    row = lambda v: v.reshape(1, -1).astype(F32)
    xs = (x_prompt.reshape(N_PROMPT, D_MODEL), x_sample.reshape(N_SAMPLE, D_MODEL))
    w_in, w_out, w_up, w_down = (w.astype(BF16) for w in (w_in, w_out, w_up, w_down))
    for l in range(DEPTH):
        lam_init = 0.8 - 0.6 * math.exp(-0.3 * l)
        qa, ka, va, qb, kb, vb, *merged = _inproj(xs, row(g_attn[l]), w_in, l, cos, sin_lo, sin_hi)
        x = merged[0] if merged else xs[0]
        ya = _na_attention(qa, ka, va, _na_col_table(rpb[l]))
        lams = (row(lam_q1[l]), row(lam_k1[l]), row(lam_q2[l]), row(lam_k2[l]))
        yb = _diff_attention(lams, row(subln_g[l]), qb, kb, vb, lam_init=lam_init)
        x = _ffn(x, ya, yb, w_out, row(g_ffn[l]), w_up, conv_w[l].astype(F32), row(conv_b[l]), w_down, row(g_final),
                 layer=l, final=(l == DEPTH - 1))
        xs = (x,)
    y_prompt, y_sample = x
    return (y_prompt.reshape(BATCH, SEQ, D_MODEL), y_sample.reshape(DEC_BATCH, DEC_SEQ, D_MODEL))
```

```python
import functools
import math

import jax
import jax.numpy as jnp
import numpy as np
from jax import lax
from jax.experimental import pallas as pl
from jax.experimental.pallas import tpu as pltpu

F32 = jnp.float32
BF16 = jnp.bfloat16

D_MODEL = 1024
BATCH, SEQ = 4, 4096
DEC_BATCH, DEC_SEQ = 2, 8192
DEPTH = 4
GRID_W = 64
WIN_H, WIN_W = 8, 16
NA_HEADS, NA_HEAD_DIM = 8, 64
DIFF_HEADS, DIFF_QK_DIM, DIFF_V_DIM = 4, 64, 128
SECTION = 512
IN_WIDTH = 6 * SECTION
D_FF = 2816
EPS = 1e-6

N_PROMPT = BATCH * SEQ
N_SAMPLE = DEC_BATCH * DEC_SEQ
N_TOK = N_PROMPT + N_SAMPLE
SEQ_BLOCKS = N_TOK // DEC_SEQ
PROMPT_BLOCKS = N_PROMPT // DEC_SEQ
assert DEC_SEQ == 2 * SEQ and N_PROMPT % DEC_SEQ == 0

LANES = 128
F32_SUBLANES = 8
VMEM_LIMIT_BYTES = 56 * 1024 * 1024

TM = 512
NA_QROWS = 4
NA_BROWS = 12
NA_Q = NA_QROWS * GRID_W
NA_K = NA_BROWS * GRID_W
NA_UNROLL = 8
NEG = -1e30
DIFF_TQ = 512
DIFF_TK = 1024
DIFF_VT_ROWS = DIFF_V_DIM + 16
SAFE_SCORE = 50.0
NORM_SLACK = 1.02
FF_CHUNK = 256
HALO = F32_SUBLANES
Y_HALO = 16

_NT = (((1,), (1,)), ((), ()))


def _const_spec(shape):
    return pl.BlockSpec(shape, lambda *_: (0,) * len(shape), pipeline_mode=pl.Buffered(1))


def _layer_spec(layer, shape):
    return pl.BlockSpec((None,) + shape, lambda *_: (layer,) + (0,) * len(shape), pipeline_mode=pl.Buffered(1))


def _params(*semantics):
    return pltpu.CompilerParams(dimension_semantics=semantics, vmem_limit_bytes=VMEM_LIMIT_BYTES)


def _rms(x, g):
    return x * lax.rsqrt(jnp.mean(x * x, axis=-1, keepdims=True) + EPS) * g


def _split_heads(x):
    first = lax.broadcasted_iota(jnp.int32, (1, LANES), 1) < LANES // 2
    zero = jnp.zeros_like(x)
    return jnp.concatenate([jnp.where(first, x, zero), jnp.where(first, zero, x)], axis=0)


def _pos_block(i):
    tiles_prompt = N_PROMPT // TM
    return jnp.where(i < tiles_prompt, i % (SEQ // TM), (i - tiles_prompt) % (DEC_SEQ // TM))


def _inproj_kernel(*refs, first_layer):
    if first_layer:
        xp_ref, xs_ref, g_ref, w_ref, cos_ref, sin_lo_ref, sin_hi_ref = refs[:7]
        qa_ref, ka_ref, va_ref, qb_ref, kb_ref, vb_ref, xm_ref = refs[7:]
        x = jnp.where(pl.program_id(0) < N_PROMPT // TM, xp_ref[...], xs_ref[...])
        xm_ref[...] = x
    else:
        x_ref, g_ref, w_ref, cos_ref, sin_lo_ref, sin_hi_ref = refs[:6]
        qa_ref, ka_ref, va_ref, qb_ref, kb_ref, vb_ref = refs[6:]
        x = x_ref[...]
    n = _rms(x, g_ref[...]).astype(BF16)

    def proj(j):
        return jnp.dot(n, w_ref[:, j * SECTION:(j + 1) * SECTION], preferred_element_type=F32)

    qa_ref[...] = (proj(0) * (1.0 / math.sqrt(NA_HEAD_DIM))).astype(BF16)
    ka_ref[...] = proj(1).astype(BF16)
    va_ref[...] = proj(2).astype(BF16)
    cos, sin_lo, sin_hi = cos_ref[...], sin_lo_ref[...], sin_hi_ref[...]

    def rotary(x, scale):
        heads = []
        for h in range(DIFF_HEADS):
            xh = x[:, h * LANES:(h + 1) * LANES]
            r = xh * cos + pltpu.roll(xh, 96, 1) * sin_lo + pltpu.roll(xh, 32, 1) * sin_hi
            heads.append((r * scale).astype(BF16))
        return jnp.concatenate(heads, axis=1)

    qb_ref[...] = rotary(proj(3), 1.0 / math.sqrt(DIFF_QK_DIM))
    kb_ref[...] = rotary(proj(4), 1.0)
    vb_ref[...] = proj(5).astype(BF16)


def _inproj(xs, g, w, layer, cos, sin_lo, sin_hi):
    first_layer = len(xs) == 2
    tiles_prompt = N_PROMPT // TM
    tok = pl.BlockSpec((TM, D_MODEL), lambda i: (i, 0))
    pos = pl.BlockSpec((TM, LANES), lambda i: (_pos_block(i), 0))
    sec = pl.BlockSpec((TM, SECTION), lambda i: (i, 0))
    if first_layer:
        x_specs = [pl.BlockSpec((TM, D_MODEL), lambda i: (jnp.minimum(i, tiles_prompt - 1), 0)),
                   pl.BlockSpec((TM, D_MODEL), lambda i: (jnp.maximum(i - tiles_prompt, 0), 0))]
    else:
        x_specs = [tok]
    return pl.pallas_call(
        functools.partial(_inproj_kernel, first_layer=first_layer),
        grid=(N_TOK // TM,),
        in_specs=x_specs + [_const_spec((1, D_MODEL)), _layer_spec(layer, (D_MODEL, IN_WIDTH)), pos, pos, pos],
        out_specs=[sec] * 6 + [tok] * first_layer,
        out_shape=([jax.ShapeDtypeStruct((N_TOK, SECTION), BF16)] * 6
                   + [jax.ShapeDtypeStruct((N_TOK, D_MODEL), F32)] * first_layer),
        compiler_params=_params("parallel"),
        name="inproj_first" if first_layer else "inproj",
    )(*xs, g, w, cos, sin_lo, sin_hi)


def _rotary_tables():
    d = DIFF_QK_DIM
    inv_freq = 1.0 / (10000.0 ** (jnp.arange(0, d, 2, dtype=F32) / d))
    ang = jnp.arange(DEC_SEQ, dtype=F32)[:, None] * inv_freq[None, :]
    cos, sin, zero = jnp.cos(ang), jnp.sin(ang), jnp.zeros_like(ang)
    tile = lambda a, b: jnp.concatenate([a, b, a, b], axis=-1)
    return tile(cos, cos), tile(-sin, zero), tile(zero, sin)


def _na_row_patterns(rows):
    nblk = rows // NA_QROWS

    def one(j):
        r0 = j * NA_QROWS
        band = int(np.clip(r0 - WIN_H // 2, 0, rows - NA_BROWS))
        r = r0 + np.arange(NA_QROWS)
        start = np.clip(r - WIN_H // 2, 0, rows - WIN_H)
        kr = band + np.arange(NA_BROWS)
        ok = (kr[None, :] >= start[:, None]) & (kr[None, :] < start[:, None] + WIN_H)
        dr = kr[None, :] - r[:, None] + (WIN_H - 1)
        return ok, np.where(ok, dr, 0)

    pats = [one(0), one(1), one(nblk - 1)]
    for j in range(1, nblk - 1):
        assert all((a == b).all() for a, b in zip(one(j), pats[1]))
    return np.stack([p[0] for p in pats]), np.stack([p[1] for p in pats])


def _na_col_table(rpb_l):
    cols = np.arange(GRID_W)
    col_start = np.clip(cols - WIN_W // 2, 0, GRID_W - WIN_W)
    ok = (cols[None, :] >= col_start[:, None]) & (cols[None, :] < col_start[:, None] + WIN_W)
    dc = cols[None, :] - cols[:, None] + (WIN_W - 1)
    onehot = ((dc[None] == np.arange(2 * WIN_W - 1)[:, None, None]) & ok[None]).astype(np.float32)
    t = jnp.einsum("hrd,dck->hrck", rpb_l.astype(F32), jnp.asarray(onehot), precision=lax.Precision.HIGHEST)
    t = jnp.where(jnp.asarray(ok), t, NEG)
    return jnp.concatenate([t, t], axis=-1)


def _na_kernel(q_ref, k_ref, v_ref, tt_ref, o_ref, bias_ref, s_ref):
    blk = pl.program_id(1)

    @pl.when(blk == 0)
    def _():
        ok, dr = _na_row_patterns(SEQ // GRID_W)
        ok2, dr2 = _na_row_patterns(DEC_SEQ // GRID_W)
        assert (ok == ok2).all() and (dr == dr2).all()
        neg = jnp.full((GRID_W, GRID_W), NEG, F32)
        for pat in range(3):
            for hh in range(2):
                for qr in range(NA_QROWS):
                    r = hh * NA_Q + qr * GRID_W
                    for w in range(NA_BROWS):
                        half = (w % 2) * GRID_W
                        blockval = tt_ref[hh, int(dr[pat, qr, w]), :, half:half + GRID_W] if ok[pat, qr, w] else neg
                        bias_ref[pat, r:r + GRID_W, w * GRID_W:(w + 1) * GRID_W] = blockval

    is_prompt = blk < PROMPT_BLOCKS
    rows = jnp.where(is_prompt, SEQ // GRID_W, DEC_SEQ // GRID_W)
    nblk_p, nblk_s = SEQ // GRID_W // NA_QROWS, DEC_SEQ // GRID_W // NA_QROWS
    first = lax.broadcasted_iota(jnp.int32, (1, LANES), 1) < NA_HEAD_DIM
    ones = jnp.ones((NA_K, LANES), BF16)

    def place(j):
        jr = jnp.where(is_prompt, j % nblk_p, j)
        base = jnp.where(is_prompt, (j // nblk_p) * (SEQ // GRID_W), 0)
        last = jnp.where(is_prompt, nblk_p, nblk_s) - 1
        r0 = jr * NA_QROWS
        band = jnp.clip(r0 - WIN_H // 2, 0, rows - NA_BROWS)
        pat = jnp.where(jr == 0, 0, jnp.where(jr == last, 2, 1))
        qoff = pl.multiple_of((base + r0) * GRID_W, NA_Q)
        koff = pl.multiple_of((base + band) * GRID_W, GRID_W)
        return qoff, koff, pat

    def scores(j, slot):
        qoff, koff, pat = place(j)
        qs = _split_heads(q_ref[pl.ds(qoff, NA_Q), :])
        s_ref[slot] = (lax.dot_general(qs, k_ref[pl.ds(koff, NA_K), :], _NT, preferred_element_type=F32)
                       + bias_ref[pat])

    def finish(j, slot, shifted):
        qoff, koff, _ = place(j)
        s = s_ref[slot]
        e = jnp.exp(s - jnp.max(s, axis=-1, keepdims=True) if shifted else s).astype(BF16)
        o = jnp.dot(e, jnp.concatenate([v_ref[pl.ds(koff, NA_K), :], ones], axis=1), preferred_element_type=F32)
        o = o[:, :LANES] / o[:, LANES:]
        o_ref[pl.ds(qoff, NA_Q), :] = jnp.where(first, o[:NA_Q], o[NA_Q:]).astype(BF16)

    lane_half = lambda axis: lax.broadcasted_iota(jnp.int32, (LANES, LANES), axis) // NA_HEAD_DIM
    same_half = jnp.where(lane_half(0) == lane_half(1), 1.0, 0.0).astype(BF16)

    def max_sq_norms(ref):
        x = ref[...]
        return jnp.max(jnp.dot(x * x, same_half, preferred_element_type=F32), axis=0, keepdims=True)

    tt = tt_ref[...]
    bias_max = jnp.max(jnp.where(tt > 0.5 * NEG, jnp.abs(tt), 0.0))
    qk_max_sq = jnp.max(max_sq_norms(q_ref) * max_sq_norms(k_ref)) * NORM_SLACK
    room = SAFE_SCORE - bias_max
    needs_max = jnp.logical_or(room <= 0.0, qk_max_sq > room * room)

    def run(shifted):
        def body(g, carry):
            for u in range(NA_UNROLL):
                j = g * NA_UNROLL + u
                scores(jnp.minimum(j + 1, nblk_s - 1), (u + 1) % 2)
                finish(j, u % 2, shifted)
            return carry

        def go():
            scores(0, 0)
            lax.fori_loop(0, nblk_s // NA_UNROLL, body, 0)
        return go

    lax.cond(needs_max, run(True), run(False))


def _na_attention(qa, ka, va, tt):
    blk = pl.BlockSpec((DEC_SEQ, LANES), lambda hp, b: (b, hp))
    return pl.pallas_call(
        _na_kernel,
        grid=(NA_HEADS // 2, SEQ_BLOCKS),
        in_specs=[blk, blk, blk,
                  pl.BlockSpec((2, 2 * WIN_H - 1, GRID_W, LANES), lambda hp, b: (hp, 0, 0, 0))],
        out_specs=blk,
        out_shape=jax.ShapeDtypeStruct((N_TOK, SECTION), BF16),
        scratch_shapes=[pltpu.VMEM((3, 2 * NA_Q, NA_K), F32), pltpu.VMEM((2, 2 * NA_Q, NA_K), F32)],
        compiler_params=_params("arbitrary", "arbitrary"),
        name="na_attention",
    )(qa, ka, va, tt)


def _diff_kernel(lq1_ref, lk1_ref, lq2_ref, lk2_ref, g_ref, q_ref, k_ref, v_ref, o_ref, s_ref, acc_ref, m_ref,
                 vt_ref, *, lam_init):
    is_prompt = pl.program_id(0) < PROMPT_BLOCKS
    n_qblocks = DEC_SEQ // DIFF_TQ

    def transpose_values(c, carry):
        rows = pl.ds(pl.multiple_of(c * DIFF_TQ, DIFF_TQ), DIFF_TQ)
        vt_ref[0:DIFF_V_DIM, rows] = v_ref[rows, :].astype(F32).T.astype(BF16)
        return carry

    lax.fori_loop(0, n_qblocks, transpose_values, 0)
    vt_ref[DIFF_V_DIM:, :] = jnp.ones((DIFF_VT_ROWS - DIFF_V_DIM, DEC_SEQ), BF16)

    def key_rows(qi, c, seq):
        base = (qi // (seq // DIFF_TQ)) * seq
        return pl.ds(pl.multiple_of(base + c * DIFF_TK, DIFF_TK), DIFF_TK)

    def scores(qi, c, slot, seq):
        q = q_ref[pl.ds(pl.multiple_of(qi * DIFF_TQ, DIFF_TQ), DIFF_TQ), :]
        s_ref[slot] = lax.dot_general(k_ref[key_rows(qi, c, seq), :], _split_heads(q), _NT,
                                      preferred_element_type=F32)

    def accumulate(qi, c, slot, seq, shifted):
        s = s_ref[slot]
        p = jnp.exp(s - m_ref[0:1, :] if shifted else s).astype(BF16)
        acc_ref[...] += jnp.dot(vt_ref[:, key_rows(qi, c, seq)], p, preferred_element_type=F32)

    def finish(qi):
        acc = acc_ref[...]
        sm1 = acc[:DIFF_V_DIM, :DIFF_TQ] / acc[DIFF_V_DIM:DIFF_V_DIM + 1, :DIFF_TQ]
        sm2 = acc[:DIFF_V_DIM, DIFF_TQ:] / acc[DIFF_V_DIM:DIFF_V_DIM + 1, DIFF_TQ:]
        lam = (jnp.exp(jnp.sum(lq1_ref[...] * lk1_ref[...], axis=-1, keepdims=True))
               - jnp.exp(jnp.sum(lq2_ref[...] * lk2_ref[...], axis=-1, keepdims=True)) + lam_init)
        d = sm1 - lam * sm2
        d = d * lax.rsqrt(jnp.mean(d * d, axis=0, keepdims=True) + EPS)
        y = d.T * g_ref[...] * (1.0 - lam_init)
        o_ref[pl.ds(pl.multiple_of(qi * DIFF_TQ, DIFF_TQ), DIFF_TQ), :] = y.astype(BF16)
        acc_ref[...] = jnp.zeros_like(acc_ref)

    lane_half = lambda axis: lax.broadcasted_iota(jnp.int32, (LANES, LANES), axis) // DIFF_QK_DIM
    same_half = jnp.where(lane_half(0) == lane_half(1), 1.0, 0.0).astype(BF16)

    def max_sq_norms(ref):
        x = ref[...]
        return jnp.max(jnp.dot(x * x, same_half, preferred_element_type=F32), axis=0, keepdims=True)

    bound_sq = jnp.max(max_sq_norms(q_ref) * max_sq_norms(k_ref)) * NORM_SLACK
    needs_max = bound_sq > SAFE_SCORE ** 2
    acc_ref[...] = jnp.zeros_like(acc_ref)

    def unshifted(seq):
        nchunks = seq // DIFF_TK
        assert nchunks % 2 == 0

        def body(qi, carry):
            for c in range(nchunks):
                if c + 1 < nchunks:
                    scores(qi, c + 1, (c + 1) % 2, seq)
                else:
                    scores(jnp.minimum(qi + 1, n_qblocks - 1), 0, 0, seq)
                accumulate(qi, c, c % 2, seq, False)
            finish(qi)
            return carry

        def run():
            scores(0, 0, 0, seq)
            lax.fori_loop(0, n_qblocks, body, 0)
        return run

    def shifted(seq):
        nchunks = seq // DIFF_TK

        def qblock(qi, carry):
            def col_max(c, m):
                scores(qi, c, 0, seq)
                return jnp.maximum(m, jnp.max(s_ref[0], axis=0, keepdims=True))

            m = lax.fori_loop(0, nchunks, col_max, jnp.full((1, 2 * DIFF_TQ), NEG, F32))
            m_ref[...] = jnp.broadcast_to(m, m_ref.shape)

            def chunk(c, carry):
                scores(qi, c, 0, seq)
                accumulate(qi, c, 0, seq, True)
                return carry

            lax.fori_loop(0, nchunks, chunk, 0)
            finish(qi)
            return carry

        def run():
            lax.fori_loop(0, n_qblocks, qblock, 0)
        return run

    lax.cond(needs_max,
             lambda: lax.cond(is_prompt, shifted(SEQ), shifted(DEC_SEQ)),
             lambda: lax.cond(is_prompt, unshifted(SEQ), unshifted(DEC_SEQ)))


def _diff_attention(lams, g, qb, kb, vb, *, lam_init):
    vec = _const_spec((1, DIFF_QK_DIM))
    blk = pl.BlockSpec((DEC_SEQ, LANES), lambda b, h: (b, h))
    return pl.pallas_call(
        functools.partial(_diff_kernel, lam_init=lam_init),
        grid=(SEQ_BLOCKS, DIFF_HEADS),
        in_specs=[vec, vec, vec, vec, _const_spec((1, DIFF_V_DIM)), blk, blk, blk],
        out_specs=blk,
        out_shape=jax.ShapeDtypeStruct((N_TOK, SECTION), BF16),
        scratch_shapes=[pltpu.VMEM((2, DIFF_TK, 2 * DIFF_TQ), F32), pltpu.VMEM((DIFF_VT_ROWS, 2 * DIFF_TQ), F32),
                        pltpu.VMEM((F32_SUBLANES, 2 * DIFF_TQ), F32), pltpu.VMEM((DIFF_VT_ROWS, DEC_SEQ), BF16)],
        compiler_params=_params("parallel", "parallel"),
        name="diff_attention",
    )(*lams, g, qb, kb, vb)


def _ffn_kernel(x_ref, xp_ref, xn_ref, ya_ref, yap_ref, yan_ref, yb_ref, ybp_ref, ybn_ref, wo_ref, g_ref, wup_ref,
                cw_ref, cb_ref, wdn_ref, gf_ref, *out_and_scratch, final):
    h_ref = out_and_scratch[-1]
    i = pl.program_id(0)
    tiles_prompt = N_PROMPT // TM
    per_seq = jnp.where(i < tiles_prompt, SEQ // TM, DEC_SEQ // TM)
    has_prev = (i % per_seq) != 0
    has_next = (i % per_seq) != per_seq - 1

    ext = TM + 2 * HALO
    ya = jnp.concatenate([yap_ref[...], ya_ref[...], yan_ref[...]], axis=0)
    yb = jnp.concatenate([ybp_ref[...], yb_ref[...], ybn_ref[...]], axis=0)
    attn = (jnp.dot(ya, wo_ref[:SECTION, :], preferred_element_type=F32)
            + jnp.dot(yb, wo_ref[SECTION:, :], preferred_element_type=F32))
    xe = (jnp.concatenate([xp_ref[...], x_ref[...], xn_ref[...]], axis=0)
          + attn[Y_HALO - HALO:Y_HALO - HALO + ext])
    x = xe[HALO:HALO + TM]
    n = _rms(xe, g_ref[...]).astype(BF16)
    row = lax.broadcasted_iota(jnp.int32, (ext, 1), 0)
    keep = ((row >= HALO) | has_prev) & ((row < HALO + TM) | has_next)

    for c in range(D_FF // FF_CHUNK):
        lo = c * FF_CHUNK
        gate = jnp.dot(n, wup_ref[:, lo:lo + FF_CHUNK], preferred_element_type=F32)
        val = jnp.dot(n, wup_ref[:, D_FF + lo:D_FF + lo + FF_CHUNK], preferred_element_type=F32)
        gate = jnp.where(keep, gate, 0.0)
        cw = cw_ref[:, lo:lo + FF_CHUNK]
        conv = (pltpu.roll(gate, 1, 0) * cw[0:1] + gate * cw[1:2] + pltpu.roll(gate, ext - 1, 0) * cw[2:3]
                + cb_ref[:, lo:lo + FF_CHUNK])
        act = 0.5 * conv * (1.0 + lax.erf(conv * (1.0 / math.sqrt(2.0))))
        h_ref[:, lo:lo + FF_CHUNK] = (act * val)[HALO:HALO + TM].astype(BF16)

    y = x + jnp.dot(h_ref[...], wdn_ref[...], preferred_element_type=F32)
    if final:
        prompt_ref, sample_ref = out_and_scratch[:2]
        res = _rms(y, gf_ref[...])

        @pl.when(i < tiles_prompt)
        def _():
            prompt_ref[...] = res

        @pl.when(i >= tiles_prompt)
        def _():
            sample_ref[...] = res
    else:
        out_and_scratch[0][...] = y


def _ffn(x, ya, yb, wo, g, wup, cw, cb, wdn, gf, *, layer, final):
    def tile_and_halos(width, halo):
        per_tile = TM // halo
        return [pl.BlockSpec((TM, width), lambda i: (i, 0)),
                pl.BlockSpec((halo, width), lambda i: (jnp.maximum(i * per_tile - 1, 0), 0)),
                pl.BlockSpec((halo, width), lambda i: (jnp.minimum((i + 1) * per_tile, N_TOK // halo - 1), 0))]

    x_specs = tile_and_halos(D_MODEL, HALO)
    y_specs = tile_and_halos(SECTION, Y_HALO)
    tiles_prompt = N_PROMPT // TM
    if final:
        out_specs = [pl.BlockSpec((TM, D_MODEL), lambda i: (jnp.minimum(i, tiles_prompt - 1), 0)),
                     pl.BlockSpec((TM, D_MODEL), lambda i: (jnp.maximum(i - tiles_prompt, 0), 0))]
        out_shape = [jax.ShapeDtypeStruct((N_PROMPT, D_MODEL), F32), jax.ShapeDtypeStruct((N_SAMPLE, D_MODEL), F32)]
    else:
        out_specs, out_shape = x_specs[0], jax.ShapeDtypeStruct((N_TOK, D_MODEL), F32)
    return pl.pallas_call(
        functools.partial(_ffn_kernel, final=final),
        grid=(N_TOK // TM,),
        in_specs=x_specs + y_specs + y_specs + [
            _layer_spec(layer, (2 * SECTION, D_MODEL)), _const_spec((1, D_MODEL)),
            _layer_spec(layer, (D_MODEL, 2 * D_FF)), _const_spec((3, D_FF)), _const_spec((1, D_FF)),
            _layer_spec(layer, (D_FF, D_MODEL)), _const_spec((1, D_MODEL))],
        out_specs=out_specs,
        out_shape=out_shape,
        scratch_shapes=[pltpu.VMEM((TM, D_FF), BF16)],
        compiler_params=_params("arbitrary"),
        name="ffn_final" if final else "ffn",
    )(x, x, x, ya, ya, ya, yb, yb, yb, wo, g, wup, cw, cb, wdn, gf)


def kernel(x_prompt, x_sample, g_attn, w_in, rpb, lam_q1, lam_k1, lam_q2, lam_k2, subln_g, w_out, g_ffn, w_up,
           conv_w, conv_b, w_down, g_final):
    cos, sin_lo, sin_hi = _rotary_tables()
    row = lambda v: v.reshape(1, -1).astype(F32)
    xs = (x_prompt.reshape(N_PROMPT, D_MODEL), x_sample.reshape(N_SAMPLE, D_MODEL))
    w_in, w_out, w_up, w_down = (w.astype(BF16) for w in (w_in, w_out, w_up, w_down))
    for l in range(DEPTH):
        lam_init = 0.8 - 0.6 * math.exp(-0.3 * l)
        qa, ka, va, qb, kb, vb, *merged = _inproj(xs, row(g_attn[l]), w_in, l, cos, sin_lo, sin_hi)
        x = merged[0] if merged else xs[0]
        ya = _na_attention(qa, ka, va, _na_col_table(rpb[l]))
        lams = (row(lam_q1[l]), row(lam_k1[l]), row(lam_q2[l]), row(lam_k2[l]))
        yb = _diff_attention(lams, row(subln_g[l]), qb, kb, vb, lam_init=lam_init)
        x = _ffn(x, ya, yb, w_out, row(g_ffn[l]), w_up, conv_w[l].astype(F32), row(conv_b[l]), w_down, row(g_final),
                 layer=l, final=(l == DEPTH - 1))
        xs = (x,)
    y_prompt, y_sample = x
    return (y_prompt.reshape(BATCH, SEQ, D_MODEL), y_sample.reshape(DEC_BATCH, DEC_SEQ, D_MODEL))
```
